```python
import jax, jax.numpy as jnp
from jax import lax
import numpy as np

D_MODEL = 2048
BATCH = 16
SEQ = 2048
DEPTH = 4

CHUNK = 64
W_A = D_MODEL // 2
W_B = D_MODEL // 2
W_C = D_MODEL // 2
CONV_A = 3
CONV_B = 31
SGU_BLOCK = 128
SGU_HEADS = 8
SGU_HEAD_DIM = W_C // SGU_HEADS
N_BRANCH = 3
IN_SPLITS = [W_A, 2 * W_A, 3 * W_A, 3 * W_A + W_B, 3 * W_A + 2 * W_B, 3 * W_A + 2 * W_B + W_C]
IN_COLS = 3 * W_A + 2 * W_B + 2 * W_C
N_EXPERTS = 32
TOP_K = 4
D_EXPERT = 3 * D_MODEL // 8
SWIGLU_LIMIT = 7.0
SWIGLU_ALPHA = 1.702
ROUTE_BLOCK = 128
N_MOD = 6
LN_EPS = 1e-5
DEEPNORM_ALPHA = (2.0 * DEPTH) ** 0.25
DEEPNORM_BETA = (8.0 * DEPTH) ** -0.25

kernel_name = "hybrid_conv_sgu_moe_streaming_encoder"


def layer_norm(x, gain, bias):
    xf = x.astype(jnp.float32)
    mu = jnp.mean(xf, axis=-1, keepdims=True)
    var = jnp.mean(jnp.square(xf - mu), axis=-1, keepdims=True)
    y = (xf - mu) * lax.rsqrt(var + LN_EPS) * gain.astype(jnp.float32) + bias.astype(jnp.float32)
    return y.astype(x.dtype)


def causal_depthwise_conv(x, w):
    k = w.shape[0]
    return lax.conv_general_dilated(
        x, w[:, None, :].astype(x.dtype), window_strides=(1,), padding=[(k - 1, 0)],
        dimension_numbers=("NWC", "WIO", "NWC"), feature_group_count=x.shape[-1])


def short_conv_mixer(b_gate, c_gate, hv, conv_w, w_out):
    y = c_gate * causal_depthwise_conv(b_gate * hv, conv_w)
    return y @ w_out


def conformer_conv(a, g, dw_w, dw_b, ln_g, ln_b, w_pw):
    y = a * jax.nn.sigmoid(g)
    y = causal_depthwise_conv(y, dw_w) + dw_b
    y = jax.nn.silu(layer_norm(y, ln_g, ln_b))
    return y @ w_pw


def sgu_mixer(u, v, ln_g, ln_b, w_s, b_s, w_out):
    u = jax.nn.gelu(u)
    v = layer_norm(jax.nn.gelu(v), ln_g, ln_b)
    bsz, seq, _ = v.shape
    nb = seq // SGU_BLOCK
    v = v.reshape(bsz, nb, SGU_BLOCK, SGU_HEADS, SGU_HEAD_DIM)
    chunk_id = jnp.arange(SGU_BLOCK) // CHUNK
    mask = chunk_id[:, None] >= chunk_id[None, :]
    w = jnp.where(mask[None], w_s, 0).astype(v.dtype)
    s = jnp.einsum("gij,bnjgd->bnigd", w, v) + b_s.T[:, :, None].astype(v.dtype)
    return (u * s.reshape(bsz, seq, W_C)) @ w_out


def moe_ffn(h, w_router, b_router, w_gu, b_gu, w_down, b_down):
    bsz, seq, d = h.shape
    xt = h.reshape(-1, d)
    n = xt.shape[0]
    nk = n * TOP_K
    logits = (xt @ w_router + b_router).astype(jnp.float32)
    top_logit, top_idx = lax.top_k(logits, TOP_K)
    top_w = jax.nn.softmax(top_logit, axis=-1)
    flat_e = top_idx.reshape(-1)
    order = jnp.argsort(flat_e)
    sorted_e = flat_e[order]
    counts = jnp.bincount(flat_e, length=N_EXPERTS)
    padded = (counts + ROUTE_BLOCK - 1) // ROUTE_BLOCK * ROUTE_BLOCK
    start = jnp.cumsum(counts) - counts
    pend = jnp.cumsum(padded)
    pstart = pend - padded
    dest = pstart[sorted_e] + jnp.arange(nk) - start[sorted_e]
    n_rows = (nk + ROUTE_BLOCK - 1) // ROUTE_BLOCK * ROUTE_BLOCK + N_EXPERTS * ROUTE_BLOCK
    n_blocks = n_rows // ROUTE_BLOCK
    tok_sorted = (order // TOP_K).astype(jnp.int32)
    row_token = jnp.full((n_rows,), n, jnp.int32).at[dest].set(tok_sorted)
    row_w = jnp.zeros((n_rows,), h.dtype).at[dest].set(top_w.reshape(-1)[order].astype(h.dtype))
    block_e = jnp.minimum(
        jnp.searchsorted(pend, jnp.arange(n_blocks) * ROUTE_BLOCK, side="right"), N_EXPERTS - 1)
    xb = xt[jnp.minimum(row_token, n - 1)].reshape(n_blocks, ROUTE_BLOCK, d)

    def expert_block(args):
        xblk, e = args
        gu = xblk @ w_gu[e] + b_gu[e]
        gate = jnp.minimum(gu[:, :D_EXPERT], SWIGLU_LIMIT)
        up = jnp.clip(gu[:, D_EXPERT:], -SWIGLU_LIMIT, SWIGLU_LIMIT)
        act = (up + 1) * (gate * jax.nn.sigmoid(SWIGLU_ALPHA * gate))
        return act @ w_down[e] + b_down[e]

    yb = lax.map(expert_block, (xb, block_e)).reshape(n_rows, d)
    y = jax.ops.segment_sum(yb * row_w[:, None], row_token, num_segments=n + 1)[:n]
    return y.reshape(bsz, seq, d)


def setup_inputs(seed: int = 0) -> dict:
    key = jax.random.key(seed)
    ks = jax.random.split(key, 32)
    L, D = DEPTH, D_MODEL

    def nrm(k, shape, s):
        return jax.random.normal(k, shape, jnp.float32) * s

    gate_offset = jnp.array([0.0, 0.0, 1.0, 0.0, 0.0, 1.0], jnp.float32)[None, :, None]
    return {
        "x": nrm(ks[0], (BATCH, SEQ, D), 1.0),
        "c": nrm(ks[1], (BATCH, D), 1.0),
        "w_ada": nrm(ks[2], (D, N_MOD * D), 0.3 * D ** -0.5),
        "b_ada": nrm(ks[3], (N_MOD * D,), 0.01),
        "ada_table": nrm(ks[4], (L, N_MOD, D), 0.1) + gate_offset,
        "w_in": nrm(ks[5], (L, D, IN_COLS), D ** -0.5),
        "conv_a": nrm(ks[6], (L, CONV_A, W_A), CONV_A ** -0.5),
        "dw_b_w": nrm(ks[7], (L, CONV_B, W_B), CONV_B ** -0.5),
        "dw_b_b": nrm(ks[8], (L, W_B), 0.01),
        "ln_b_g": 1.0 + nrm(ks[9], (L, W_B), 0.02),
        "ln_b_b": nrm(ks[10], (L, W_B), 0.01),
        "ln_c_g": 1.0 + nrm(ks[11], (L, W_C), 0.02),
        "ln_c_b": nrm(ks[12], (L, W_C), 0.01),
        "sgu_w": nrm(ks[13], (L, SGU_HEADS, SGU_BLOCK, SGU_BLOCK), SGU_BLOCK ** -0.5),
        "sgu_b": 1.0 + nrm(ks[14], (L, SGU_HEADS, SGU_BLOCK), 0.1),
        "w_out_a": nrm(ks[15], (L, W_A, D), DEEPNORM_BETA * W_A ** -0.5),
        "w_out_b": nrm(ks[16], (L, W_B, D), DEEPNORM_BETA * W_B ** -0.5),
        "w_out_c": nrm(ks[17], (L, W_C, D), DEEPNORM_BETA * W_C ** -0.5),
        "w_gate": nrm(ks[18], (L, D, N_BRANCH * D), D ** -0.5),
        "b_gate": nrm(ks[19], (L, N_BRANCH * D), 0.01),
        "w_o": nrm(ks[20], (L, D, D), DEEPNORM_BETA * D ** -0.5),
        "ln1_g": 1.0 + nrm(ks[21], (L, D), 0.02),
        "ln1_b": nrm(ks[22], (L, D), 0.01),
        "w_router": nrm(ks[23], (L, D, N_EXPERTS), D ** -0.5),
        "b_router": nrm(ks[24], (L, N_EXPERTS), 0.01),
        "w_gu": nrm(ks[25], (L, N_EXPERTS, D, 2 * D_EXPERT), D ** -0.5),
        "b_gu": nrm(ks[26], (L, N_EXPERTS, 2 * D_EXPERT), 0.01),
        "w_down": nrm(ks[27], (L, N_EXPERTS, D_EXPERT, D), DEEPNORM_BETA * D_EXPERT ** -0.5),
        "b_down": nrm(ks[28], (L, N_EXPERTS, D), 0.01),
        "ln2_g": 1.0 + nrm(ks[29], (L, D), 0.02),
        "ln2_b": nrm(ks[30], (L, D), 0.01),
    }


def reference(x, c, w_ada, b_ada, ada_table, w_in, conv_a, dw_b_w, dw_b_b, ln_b_g, ln_b_b,
              ln_c_g, ln_c_b, sgu_w, sgu_b, w_out_a, w_out_b, w_out_c, w_gate, b_gate, w_o,
              ln1_g, ln1_b, w_router, b_router, w_gu, b_gu, w_down, b_down, ln2_g, ln2_b):
    bsz, seq, d = x.shape
    mod_shared = (jax.nn.silu(c) @ w_ada + b_ada).reshape(bsz, N_MOD, d)
    for l in range(DEPTH):
        mod = mod_shared + ada_table[l]
        sh1, sc1, g1 = mod[:, 0, None, :], mod[:, 1, None, :], mod[:, 2, None, :]
        sh2, sc2, g2 = mod[:, 3, None, :], mod[:, 4, None, :], mod[:, 5, None, :]

        h = x * (1 + sc1) + sh1
        proj = h @ w_in[l]
        b_a, c_a, h_a, a_b, g_b, u_c, v_c = jnp.split(proj, IN_SPLITS, axis=-1)
        y_a = short_conv_mixer(b_a, c_a, h_a, conv_a[l], w_out_a[l])
        y_b = conformer_conv(a_b, g_b, dw_b_w[l], dw_b_b[l], ln_b_g[l], ln_b_b[l], w_out_b[l])
        y_c = sgu_mixer(u_c, v_c, ln_c_g[l], ln_c_b[l], sgu_w[l], sgu_b[l], w_out_c[l])
        gates = jax.nn.sigmoid(h @ w_gate[l] + b_gate[l]).reshape(bsz, seq, N_BRANCH, d)
        merged = gates[:, :, 0] * y_a + gates[:, :, 1] * y_b + gates[:, :, 2] * y_c
        x = layer_norm(DEEPNORM_ALPHA * x + g1 * (merged @ w_o[l]), ln1_g[l], ln1_b[l])

        h = x * (1 + sc2) + sh2
        y = moe_ffn(h, w_router[l], b_router[l], w_gu[l], b_gu[l], w_down[l], b_down[l])
        x = layer_norm(DEEPNORM_ALPHA * x + g2 * y, ln2_g[l], ln2_b[l])
    return x
```

```python
import functools

import jax
import jax.numpy as jnp
from jax import lax
from jax.experimental import pallas as pl
from jax.experimental.pallas import tpu as pltpu

F32 = jnp.float32
BF16 = jnp.bfloat16

CHUNK = 64
SGU_BLOCK = 128
TOP_K = 4
SWIGLU_LIMIT = 7.0
SWIGLU_ALPHA = 1.702
LN_EPS = 1e-5
N_MOD = 6

LANES = 128
CONV_HALO = 32
ROUTER_LANES = LANES
VMEM_LIMIT = 56 * 1024 * 1024

SEQ_TILE = 256
PROJ_ROWS = 1024
MOE_ROWS = 512


def _layer_norm(x, g, b):
    mu = jnp.mean(x, axis=-1, keepdims=True)
    xc = x - mu
    var = jnp.mean(xc * xc, axis=-1, keepdims=True)
    return xc * lax.rsqrt(var + LN_EPS) * g + b


def _gelu(x):
    c = 0.7978845608028654
    return 0.5 * x * (1.0 + jnp.tanh(c * (x + 0.044715 * (x * x * x))))


def _sigmoid(x):
    return 1.0 / (1.0 + jnp.exp(-x))


def _ada_kernel(c_ref, w_ref, b_ref, o_ref):
    c = c_ref[...]
    a = c * _sigmoid(c)
    o_ref[...] = jnp.dot(a, w_ref[...], preferred_element_type=F32,
                         precision=lax.Precision.HIGHEST) + b_ref[...]


def _ada_call(c, w_ada, b_ada):
    bsz, d = c.shape
    nc = w_ada.shape[1]
    bn = min(nc, 1024)
    return pl.pallas_call(
        _ada_kernel,
        out_shape=jax.ShapeDtypeStruct((bsz, nc), F32),
        grid=(nc // bn,),
        in_specs=[pl.BlockSpec((bsz, d), lambda j: (0, 0)),
                  pl.BlockSpec((d, bn), lambda j: (0, j)),
                  pl.BlockSpec((1, bn), lambda j: (0, j))],
        out_specs=pl.BlockSpec((bsz, bn), lambda j: (0, j)),
        compiler_params=pltpu.CompilerParams(dimension_semantics=("parallel",),
                                             vmem_limit_bytes=VMEM_LIMIT),
        name="ada_proj",
    )(c, w_ada, b_ada.reshape(1, nc))


def _proj_kernel(x_ref, mod_ref, w_ref, o_ref, h_ref):
    @pl.when(pl.program_id(2) == 0)
    def _():
        sh = mod_ref[0, 0:1, :]
        sc = mod_ref[0, 1:2, :]
        h_ref[...] = (x_ref[0] * (1.0 + sc) + sh).astype(BF16)

    o_ref[0] = jnp.dot(h_ref[...], w_ref[...], preferred_element_type=F32).astype(o_ref.dtype)


def _proj_call(x, mod, w_cat, layer, bn):
    bsz, seq, d = x.shape
    nc = w_cat.shape[-1]
    bm = min(seq, PROJ_ROWS)
    return pl.pallas_call(
        _proj_kernel,
        out_shape=jax.ShapeDtypeStruct((bsz, seq, nc), BF16),
        grid=(bsz, seq // bm, nc // bn),
        in_specs=[pl.BlockSpec((1, bm, d), lambda b, i, j: (b, i, 0)),
                  pl.BlockSpec((1, N_MOD, d), lambda b, i, j: (b, 0, 0)),
                  pl.BlockSpec((None, d, bn), lambda b, i, j: (layer, 0, j))],
        out_specs=pl.BlockSpec((1, bm, bn), lambda b, i, j: (b, i, j)),
        scratch_shapes=[pltpu.VMEM((bm, d), BF16)],
        compiler_params=pltpu.CompilerParams(
            dimension_semantics=("parallel", "parallel", "arbitrary"),
            vmem_limit_bytes=VMEM_LIMIT),
        name="mixer_in_proj",
    )(x, mod, w_cat)


def _mixer_kernel(proj_ref, halo_ref, x_ref, mod_ref, ca_ref, dww_ref, dwb_ref,
                  lnbg_ref, lnbb_ref, lncg_ref, lncb_ref, sw_ref, sb_ref,
                  woa_ref, wob_ref, woc_ref, bg_ref, wo_ref, ln1g_ref, ln1b_ref,
                  wr_ref, br_ref,
                  xo_ref, h2_ref, lg_ref,
                  abuf, bbuf, cbuf, *, alpha):
    t = x_ref.shape[1]
    d = x_ref.shape[2]
    w = ca_ref.shape[1]
    n_heads = sw_ref.shape[0]
    hd = w // n_heads
    ka = ca_ref.shape[0]
    kb = dww_ref.shape[0]
    hal = CONV_HALO

    def pf(k):
        return proj_ref[0, :, k * w:(k + 1) * w].astype(F32)

    def hf(k):
        return halo_ref[0, :, k * w:(k + 1) * w].astype(F32)

    keep = jnp.where(pl.program_id(1) > 0, 1.0, 0.0).astype(F32)

    abuf[0:hal, :] = hf(0) * hf(2) * keep
    abuf[hal:hal + t, :] = pf(0) * pf(2)
    conv_a = ca_ref[0:1, :] * abuf[hal - ka + 1:hal - ka + 1 + t, :]
    for k in range(1, ka):
        conv_a = conv_a + ca_ref[k:k + 1, :] * abuf[hal - ka + 1 + k:hal - ka + 1 + k + t, :]
    y_a = (pf(1) * conv_a).astype(BF16)
    o_a = jnp.dot(y_a, woa_ref[...], preferred_element_type=F32)

    bbuf[0:hal, :] = hf(3) * _sigmoid(hf(4)) * keep
    bbuf[hal:hal + t, :] = pf(3) * _sigmoid(pf(4))
    rows = 32
    cols = 2 * LANES if w % (2 * LANES) == 0 else w
    for r0 in range(0, t, rows):
        for c0 in range(0, w, cols):
            base = hal - kb + 1 + r0
            acc = dww_ref[0:1, c0:c0 + cols] * bbuf[base:base + rows, c0:c0 + cols]
            for k in range(1, kb):
                acc = acc + dww_ref[k:k + 1, c0:c0 + cols] * bbuf[base + k:base + k + rows, c0:c0 + cols]
            cbuf[r0:r0 + rows, c0:c0 + cols] = acc
    y_b = _layer_norm(cbuf[...] + dwb_ref[...], lnbg_ref[...], lnbb_ref[...])
    y_b = (y_b * _sigmoid(y_b)).astype(BF16)
    o_b = jnp.dot(y_b, wob_ref[...], preferred_element_type=F32)

    v = _layer_norm(_gelu(pf(6)), lncg_ref[...], lncb_ref[...]).astype(BF16)
    ri = lax.broadcasted_iota(jnp.int32, (SGU_BLOCK, SGU_BLOCK), 0) // CHUNK
    ci = lax.broadcasted_iota(jnp.int32, (SGU_BLOCK, SGU_BLOCK), 1) // CHUNK
    causal = ri >= ci
    for g in range(n_heads):
        wm = jnp.where(causal, sw_ref[g], 0.0).astype(BF16)
        for blk in range(t // SGU_BLOCK):
            r0 = blk * SGU_BLOCK
            cbuf[r0:r0 + SGU_BLOCK, g * hd:(g + 1) * hd] = (
                jnp.dot(wm, v[r0:r0 + SGU_BLOCK, g * hd:(g + 1) * hd], preferred_element_type=F32)
                + sb_ref[:, g * hd:(g + 1) * hd])
    y_c = (_gelu(pf(5)) * cbuf[...]).astype(BF16)
    o_c = jnp.dot(y_c, woc_ref[...], preferred_element_type=F32)

    g0 = 7 * w

    def gate(k):
        z = proj_ref[0, :, g0 + k * d:g0 + (k + 1) * d].astype(F32) + bg_ref[:, k * d:(k + 1) * d]
        return _sigmoid(z)

    merged = (gate(0) * o_a + gate(1) * o_b + gate(2) * o_c).astype(BF16)
    mix = jnp.dot(merged, wo_ref[...], preferred_element_type=F32)
    xn = _layer_norm(alpha * x_ref[0] + mod_ref[0, 2:3, :] * mix, ln1g_ref[...], ln1b_ref[...])
    xo_ref[0] = xn

    h2 = xn * (1.0 + mod_ref[0, 4:5, :]) + mod_ref[0, 3:4, :]
    h2_ref[0] = h2.astype(BF16)
    lg_ref[0] = jnp.dot(h2, wr_ref[...], preferred_element_type=F32,
                        precision=lax.Precision.HIGHEST) + br_ref[...]


def _const_spec(shape, index):
    return pl.BlockSpec(shape, index, pipeline_mode=pl.Buffered(1))


def _mixer_call(proj, x, mod, p, layer, alpha):
    bsz, seq, d = x.shape
    w = p["conv_a"].shape[-1]
    nc = proj.shape[-1]
    t = min(seq, SEQ_TILE)
    hal = CONV_HALO
    hb = t // hal
    n_heads = p["sgu_w"].shape[1]
    row = lambda a: a[layer].reshape(1, -1)
    grid = (bsz, seq // t)
    cst = lambda *shape: _const_spec(shape, lambda b, s: (0,) * len(shape))
    lyr = lambda *shape: _const_spec((None,) + shape, lambda b, s: (layer,) + (0,) * len(shape))
    in_specs = [
        pl.BlockSpec((1, t, nc), lambda b, s: (b, s, 0)),
        pl.BlockSpec((1, hal, 7 * w), lambda b, s: (b, jnp.maximum(s * hb - 1, 0), 0)),
        pl.BlockSpec((1, t, d), lambda b, s: (b, s, 0)),
        pl.BlockSpec((1, N_MOD, d), lambda b, s: (b, 0, 0)),
        lyr(p["conv_a"].shape[1], w),
        lyr(p["dw_b_w"].shape[1], w),
        cst(1, w), cst(1, w), cst(1, w), cst(1, w), cst(1, w),
        lyr(n_heads, SGU_BLOCK, SGU_BLOCK),
        cst(SGU_BLOCK, w),
        lyr(w, d), lyr(w, d), lyr(w, d),
        cst(1, 3 * d),
        lyr(d, d),
        cst(1, d), cst(1, d),
        cst(d, ROUTER_LANES), cst(1, ROUTER_LANES),
    ]
    out_specs = [
        pl.BlockSpec((1, t, d), lambda b, s: (b, s, 0)),
        pl.BlockSpec((1, t, d), lambda b, s: (b, s, 0)),
        pl.BlockSpec((1, t, ROUTER_LANES), lambda b, s: (b, s, 0)),
    ]
    out_shape = [
        jax.ShapeDtypeStruct((bsz, seq, d), F32),
        jax.ShapeDtypeStruct((bsz, seq, d), BF16),
        jax.ShapeDtypeStruct((bsz, seq, ROUTER_LANES), F32),
    ]
    n_exp = p["w_router"].shape[-1]
    wr = jnp.pad(p["w_router"][layer], ((0, 0), (0, ROUTER_LANES - n_exp)))
    br = jnp.pad(p["b_router"][layer], (0, ROUTER_LANES - n_exp)).reshape(1, -1)
    sb = jnp.repeat(p["sgu_b"][layer].T, w // n_heads, axis=1)
    return pl.pallas_call(
        functools.partial(_mixer_kernel, alpha=alpha),
        out_shape=out_shape,
        grid=grid,
        in_specs=in_specs,
        out_specs=out_specs,
        scratch_shapes=[pltpu.VMEM((t + hal, w), F32), pltpu.VMEM((t + hal, w), F32),
                        pltpu.VMEM((t, w), F32)],
        compiler_params=pltpu.CompilerParams(dimension_semantics=("parallel", "parallel"),
                                             vmem_limit_bytes=VMEM_LIMIT),
        name="mixer",
    )(proj, proj, x, mod, p["conv_a"], p["dw_b_w"], row(p["dw_b_b"]),
      row(p["ln_b_g"]), row(p["ln_b_b"]), row(p["ln_c_g"]), row(p["ln_c_b"]),
      p["sgu_w"], sb, p["w_out_a"], p["w_out_b"], p["w_out_c"], row(p["b_gate"]),
      p["w_o"], row(p["ln1_g"]), row(p["ln1_b"]), wr, br)


def _moe_kernel(be_ref, nb_ref, x_ref, wgu_ref, bgu_ref, wd_ref, bd_ref, o_ref):
    f = wd_ref.shape[0]
    i = pl.program_id(0)

    @pl.when(i < nb_ref[0])
    def _():
        gu = jnp.dot(x_ref[...], wgu_ref[...], preferred_element_type=F32) + bgu_ref[0]
        gate = jnp.minimum(gu[:, :f], SWIGLU_LIMIT)
        up = jnp.clip(gu[:, f:], -SWIGLU_LIMIT, SWIGLU_LIMIT)
        act = (up + 1.0) * (gate * _sigmoid(SWIGLU_ALPHA * gate))
        y = jnp.dot(act.astype(BF16), wd_ref[...], preferred_element_type=F32) + bd_ref[0]
        o_ref[...] = y.astype(o_ref.dtype)

    @pl.when(i >= nb_ref[0])
    def _():
        o_ref[...] = jnp.zeros(o_ref.shape, o_ref.dtype)


def _moe_call(block_e, n_used, xb, w_gu, b_gu, w_down, b_down, layer, tm):
    n_rows, d = xb.shape
    n_exp, _, f2 = w_gu.shape[1:]
    f = f2 // 2
    grid_spec = pltpu.PrefetchScalarGridSpec(
        num_scalar_prefetch=2,
        grid=(n_rows // tm,),
        in_specs=[
            pl.BlockSpec((tm, d), lambda i, be, nb: (i, 0)),
            pl.BlockSpec((None, None, d, f2), lambda i, be, nb: (layer, be[i], 0, 0)),
            pl.BlockSpec((None, None, 1, f2), lambda i, be, nb: (layer, be[i], 0, 0)),
            pl.BlockSpec((None, None, f, d), lambda i, be, nb: (layer, be[i], 0, 0)),
            pl.BlockSpec((None, None, 1, d), lambda i, be, nb: (layer, be[i], 0, 0)),
        ],
        out_specs=pl.BlockSpec((tm, d), lambda i, be, nb: (i, 0)),
    )
    return pl.pallas_call(
        _moe_kernel,
        out_shape=jax.ShapeDtypeStruct((n_rows, d), BF16),
        grid_spec=grid_spec,
        compiler_params=pltpu.CompilerParams(dimension_semantics=("arbitrary",),
                                             vmem_limit_bytes=VMEM_LIMIT),
        name="moe_experts",
    )(block_e, n_used, xb, w_gu, b_gu.reshape(b_gu.shape[0], n_exp, 1, f2),
      w_down, b_down.reshape(b_down.shape[0], n_exp, 1, d))


def _ln2_kernel(x_ref, yg_ref, tw_ref, mod_ref, g_ref, b_ref, o_ref, *, alpha):
    tw = tw_ref[0]
    y = tw[:, 0:1] * yg_ref[0, 0].astype(F32)
    for k in range(1, TOP_K):
        y = y + tw[:, k:k + 1] * yg_ref[k, 0].astype(F32)
    o_ref[0] = _layer_norm(alpha * x_ref[0] + mod_ref[0, 5:6, :] * y, g_ref[...], b_ref[...])


def _ln2_call(x, yg, top_w, mod, g, b, alpha):
    bsz, seq, d = x.shape
    t = min(seq, SEQ_TILE)
    return pl.pallas_call(
        functools.partial(_ln2_kernel, alpha=alpha),
        out_shape=jax.ShapeDtypeStruct((bsz, seq, d), F32),
        grid=(bsz, seq // t),
        in_specs=[pl.BlockSpec((1, t, d), lambda b_, s: (b_, s, 0)),
                  pl.BlockSpec((TOP_K, 1, t, d), lambda b_, s: (0, b_, s, 0)),
                  pl.BlockSpec((1, t, TOP_K), lambda b_, s: (b_, s, 0)),
                  pl.BlockSpec((1, N_MOD, d), lambda b_, s: (b_, 0, 0)),
                  pl.BlockSpec((1, d), lambda b_, s: (0, 0)),
                  pl.BlockSpec((1, d), lambda b_, s: (0, 0))],
        out_specs=pl.BlockSpec((1, t, d), lambda b_, s: (b_, s, 0)),
        compiler_params=pltpu.CompilerParams(dimension_semantics=("parallel", "parallel"),
                                             vmem_limit_bytes=VMEM_LIMIT),
        name="moe_combine_ln",
    )(x, yg, top_w, mod, g.reshape(1, d), b.reshape(1, d))


def _route(logits, n_exp, tm):
    n = logits.shape[0]
    nk = n * TOP_K
    top_logit, top_idx = lax.top_k(logits[:, :n_exp], TOP_K)
    top_w = jax.nn.softmax(top_logit, axis=-1)
    onehot = (top_idx[:, :, None] == jnp.arange(n_exp)[None, None, :]).astype(jnp.int32).sum(1)
    incl = jnp.cumsum(onehot, axis=0)
    counts = incl[-1]
    rank = incl - onehot
    padded = (counts + tm - 1) // tm * tm
    pend = jnp.cumsum(padded)
    pstart = pend - padded
    dest = pstart[top_idx] + jnp.take_along_axis(rank, top_idx, axis=1)
    n_rows = (nk + tm - 1) // tm * tm + n_exp * tm
    n_blocks = n_rows // tm
    tok = jnp.broadcast_to(jnp.arange(n, dtype=jnp.int32)[:, None], (n, TOP_K))
    row_token = jnp.zeros((n_rows,), jnp.int32).at[dest.reshape(-1)].set(tok.reshape(-1))
    n_used = (pend[-1] // tm).astype(jnp.int32)
    blk = jnp.minimum(jnp.arange(n_blocks, dtype=jnp.int32), n_used - 1)
    block_e = jnp.minimum(jnp.searchsorted(pend, blk * tm, side="right"), n_exp - 1).astype(jnp.int32)
    return top_w, dest.astype(jnp.int32), row_token, block_e, n_used.reshape(1)


def kernel(x, c, w_ada, b_ada, ada_table, w_in, conv_a, dw_b_w, dw_b_b, ln_b_g, ln_b_b, ln_c_g, ln_c_b, sgu_w, sgu_b, w_out_a, w_out_b, w_out_c, w_gate, b_gate, w_o, ln1_g, ln1_b, w_router, b_router, w_gu, b_gu, w_down, b_down, ln2_g, ln2_b):
    bsz, seq, d = x.shape
    depth = w_in.shape[0]
    n_exp = w_router.shape[-1]
    w = conv_a.shape[-1]
    alpha = (2.0 * depth) ** 0.25
    n = bsz * seq

    mod_shared = _ada_call(c, w_ada, b_ada).reshape(bsz, N_MOD, d)
    w_cat = jnp.concatenate([w_in, w_gate], axis=-1).astype(BF16)
    p = dict(conv_a=conv_a, dw_b_w=dw_b_w, dw_b_b=dw_b_b, ln_b_g=ln_b_g, ln_b_b=ln_b_b,
             ln_c_g=ln_c_g, ln_c_b=ln_c_b, sgu_w=sgu_w, sgu_b=sgu_b,
             w_out_a=w_out_a.astype(BF16), w_out_b=w_out_b.astype(BF16),
             w_out_c=w_out_c.astype(BF16), b_gate=b_gate, w_o=w_o.astype(BF16),
             ln1_g=ln1_g, ln1_b=ln1_b, w_router=w_router, b_router=b_router)
    w_gu_b = w_gu.astype(BF16)
    w_down_b = w_down.astype(BF16)
    tm = MOE_ROWS

    for l in range(depth):
        mod = mod_shared + ada_table[l][None]
        proj = _proj_call(x, mod, w_cat, l, w)
        x, h2, logits = _mixer_call(proj, x, mod, p, l, alpha)

        top_w, dest, row_token, block_e, n_used = _route(logits.reshape(n, -1), n_exp, tm)
        xb = h2.reshape(n, d)[row_token]
        yb = _moe_call(block_e, n_used, xb, w_gu_b, b_gu, w_down_b, b_down, l, tm)
        yg = yb[dest.T].reshape(TOP_K, bsz, seq, d)
        x = _ln2_call(x, yg, top_w.reshape(bsz, seq, TOP_K), mod, ln2_g[l], ln2_b[l], alpha)
    return x
```

```python
import functools

import jax
import jax.numpy as jnp
from jax import lax
from jax.experimental import pallas as pl
from jax.experimental.pallas import tpu as pltpu

F32 = jnp.float32
BF16 = jnp.bfloat16

CHUNK = 64
SGU_BLOCK = 128
TOP_K = 4
SWIGLU_LIMIT = 7.0
SWIGLU_ALPHA = 1.702
LN_EPS = 1e-5
N_MOD = 6

LANES = 128
SUBLANES = 8
CONV_ROWS = 64
CONV_COLS = 2 * LANES
CONV_HALO = 32
ROUTER_LANES = LANES
VMEM_LIMIT = 56 * 1024 * 1024

SEQ_TILE = 256
PROJ_ROWS = 1024
MOE_ROWS = 512


def _layer_norm(x, g, b):
    mu = jnp.mean(x, axis=-1, keepdims=True)
    xc = x - mu
    var = jnp.mean(xc * xc, axis=-1, keepdims=True)
    return xc * lax.rsqrt(var + LN_EPS) * g + b


def _gelu(x):
    c = 0.7978845608028654
    return 0.5 * x * (1.0 + jnp.tanh(c * (x + 0.044715 * (x * x * x))))


def _sigmoid(x):
    return 1.0 / (1.0 + jnp.exp(-x))


def _cast_kernel(x_ref, o_ref):
    o_ref[...] = x_ref[...].astype(o_ref.dtype)


def _cast_call(a):
    n_l, n_e, r, c = a.shape
    spec = pl.BlockSpec((None, None, r, c), lambda i, j: (i, j, 0, 0))
    return pl.pallas_call(
        _cast_kernel,
        out_shape=jax.ShapeDtypeStruct(a.shape, BF16),
        grid=(n_l, n_e),
        in_specs=[spec],
        out_specs=spec,
        compiler_params=pltpu.CompilerParams(dimension_semantics=("parallel", "parallel"),
                                             vmem_limit_bytes=VMEM_LIMIT),
        name="expert_weight_cast",
    )(a)


def _ada_kernel(c_ref, w_ref, b_ref, o_ref):
    c = c_ref[...]
    a = c * _sigmoid(c)
    o_ref[...] = jnp.dot(a, w_ref[...], preferred_element_type=F32,
                         precision=lax.Precision.HIGHEST) + b_ref[...]


def _ada_call(c, w_ada, b_ada):
    bsz, d = c.shape
    nc = w_ada.shape[1]
    bn = min(nc, 1024)
    return pl.pallas_call(
        _ada_kernel,
        out_shape=jax.ShapeDtypeStruct((bsz, nc), F32),
        grid=(nc // bn,),
        in_specs=[pl.BlockSpec((bsz, d), lambda j: (0, 0)),
                  pl.BlockSpec((d, bn), lambda j: (0, j)),
                  pl.BlockSpec((1, bn), lambda j: (0, j))],
        out_specs=pl.BlockSpec((bsz, bn), lambda j: (0, j)),
        compiler_params=pltpu.CompilerParams(dimension_semantics=("parallel",),
                                             vmem_limit_bytes=VMEM_LIMIT),
        name="ada_proj",
    )(c, w_ada, b_ada.reshape(1, nc))


def _proj_kernel(x_ref, mod_ref, w_ref, o_ref, h_ref):
    @pl.when(pl.program_id(2) == 0)
    def _():
        sh = mod_ref[0, 0:1, :]
        sc = mod_ref[0, 1:2, :]
        h_ref[...] = (x_ref[0] * (1.0 + sc) + sh).astype(BF16)

    o_ref[0] = jnp.dot(h_ref[...], w_ref[...], preferred_element_type=F32).astype(o_ref.dtype)


def _proj_call(x, mod, w_cat, layer, bn):
    bsz, seq, d = x.shape
    nc = w_cat.shape[-1]
    bm = min(seq, PROJ_ROWS)
    return pl.pallas_call(
        _proj_kernel,
        out_shape=jax.ShapeDtypeStruct((bsz, seq, nc), BF16),
        grid=(bsz, seq // bm, nc // bn),
        in_specs=[pl.BlockSpec((1, bm, d), lambda b, i, j: (b, i, 0)),
                  pl.BlockSpec((1, N_MOD, d), lambda b, i, j: (b, 0, 0)),
                  pl.BlockSpec((None, d, bn), lambda b, i, j: (layer, 0, j))],
        out_specs=pl.BlockSpec((1, bm, bn), lambda b, i, j: (b, i, j)),
        scratch_shapes=[pltpu.VMEM((bm, d), BF16)],
        compiler_params=pltpu.CompilerParams(
            dimension_semantics=("parallel", "parallel", "arbitrary"),
            vmem_limit_bytes=VMEM_LIMIT),
        name="mixer_in_proj",
    )(x, mod, w_cat)


def _mixer_kernel(proj_ref, halo_ref, x_ref, mod_ref, ca_ref, dww_ref, dwb_ref,
                  lnbg_ref, lnbb_ref, lncg_ref, lncb_ref, sw_ref, sb_ref,
                  woa_ref, wob_ref, woc_ref, bg_ref, wo_ref, ln1g_ref, ln1b_ref,
                  wr_ref, br_ref,
                  xo_ref, h2_ref, lg_ref,
                  abuf, bbuf, cbuf, sbuf, *, alpha):
    t = x_ref.shape[1]
    d = x_ref.shape[2]
    w = ca_ref.shape[1]
    n_heads = sw_ref.shape[0]
    hd = w // n_heads
    ka = ca_ref.shape[0]
    kb = dww_ref.shape[0]
    hal = CONV_HALO

    def pf(k):
        return proj_ref[0, :, k * w:(k + 1) * w].astype(F32)

    def hf(k):
        return halo_ref[0, :, k * w:(k + 1) * w].astype(F32)

    keep = jnp.where(pl.program_id(1) > 0, 1.0, 0.0).astype(F32)

    abuf[0:hal, :] = hf(0) * hf(2) * keep
    abuf[hal:hal + t, :] = pf(0) * pf(2)
    conv_a = ca_ref[0:1, :] * abuf[hal - ka + 1:hal - ka + 1 + t, :]
    for k in range(1, ka):
        conv_a = conv_a + ca_ref[k:k + 1, :] * abuf[hal - ka + 1 + k:hal - ka + 1 + k + t, :]
    y_a = (pf(1) * conv_a).astype(BF16)
    o_a = jnp.dot(y_a, woa_ref[...], preferred_element_type=F32)

    bbuf[0:hal, :] = hf(3) * _sigmoid(hf(4)) * keep
    bbuf[hal:hal + t, :] = pf(3) * _sigmoid(pf(4))
    rows = CONV_ROWS
    cols = sbuf.shape[2]
    span = sbuf.shape[1]
    for c0 in range(0, w, cols):
        for r in range(1, SUBLANES):
            sbuf[r - 1] = bbuf[r:r + span, c0:c0 + cols]
        for r0 in range(0, t, rows):
            acc = None
            for k in range(kb):
                q, r = divmod(hal - kb + 1 + k, SUBLANES)
                lo = r0 + SUBLANES * q
                if r == 0:
                    src = bbuf[lo:lo + rows, c0:c0 + cols]
                else:
                    src = sbuf[r - 1, lo:lo + rows, :]
                term = dww_ref[k:k + 1, c0:c0 + cols] * src
                acc = term if acc is None else acc + term
            cbuf[r0:r0 + rows, c0:c0 + cols] = acc
    y_b = _layer_norm(cbuf[...] + dwb_ref[...], lnbg_ref[...], lnbb_ref[...])
    y_b = (y_b * _sigmoid(y_b)).astype(BF16)
    o_b = jnp.dot(y_b, wob_ref[...], preferred_element_type=F32)

    v = _layer_norm(_gelu(pf(6)), lncg_ref[...], lncb_ref[...]).astype(BF16)
    ri = lax.broadcasted_iota(jnp.int32, (SGU_BLOCK, SGU_BLOCK), 0) // CHUNK
    ci = lax.broadcasted_iota(jnp.int32, (SGU_BLOCK, SGU_BLOCK), 1) // CHUNK
    causal = ri >= ci
    for g in range(n_heads):
        wm = jnp.where(causal, sw_ref[g], 0.0).astype(BF16)
        for blk in range(t // SGU_BLOCK):
            r0 = blk * SGU_BLOCK
            cbuf[r0:r0 + SGU_BLOCK, g * hd:(g + 1) * hd] = (
                jnp.dot(wm, v[r0:r0 + SGU_BLOCK, g * hd:(g + 1) * hd], preferred_element_type=F32)
                + sb_ref[:, g * hd:(g + 1) * hd])
    y_c = (_gelu(pf(5)) * cbuf[...]).astype(BF16)
    o_c = jnp.dot(y_c, woc_ref[...], preferred_element_type=F32)

    g0 = 7 * w

    def gate(k):
        z = proj_ref[0, :, g0 + k * d:g0 + (k + 1) * d].astype(F32) + bg_ref[:, k * d:(k + 1) * d]
        return _sigmoid(z)

    merged = (gate(0) * o_a + gate(1) * o_b + gate(2) * o_c).astype(BF16)
    mix = jnp.dot(merged, wo_ref[...], preferred_element_type=F32)
    xn = _layer_norm(alpha * x_ref[0] + mod_ref[0, 2:3, :] * mix, ln1g_ref[...], ln1b_ref[...])
    xo_ref[0] = xn

    h2 = xn * (1.0 + mod_ref[0, 4:5, :]) + mod_ref[0, 3:4, :]
    h2_hi = h2.astype(BF16)
    h2_ref[0] = h2_hi
    h2_lo = (h2 - h2_hi.astype(F32)).astype(BF16)
    p_hi = jnp.dot(h2_hi, wr_ref[...], preferred_element_type=F32)
    p_lo = jnp.dot(h2_lo, wr_ref[:, 0:ROUTER_LANES], preferred_element_type=F32)
    lg_ref[0] = p_hi[:, 0:ROUTER_LANES] + p_hi[:, ROUTER_LANES:] + p_lo + br_ref[...]


def _const_spec(shape, index):
    return pl.BlockSpec(shape, index, pipeline_mode=pl.Buffered(1))


def _mixer_call(proj, x, mod, p, layer, alpha):
    bsz, seq, d = x.shape
    w = p["conv_a"].shape[-1]
    nc = proj.shape[-1]
    t = min(seq, SEQ_TILE)
    hal = CONV_HALO
    hb = t // hal
    n_heads = p["sgu_w"].shape[1]
    row = lambda a: a[layer].reshape(1, -1)
    grid = (bsz, seq // t)
    cst = lambda *shape: _const_spec(shape, lambda b, s: (0,) * len(shape))
    lyr = lambda *shape: _const_spec((None,) + shape, lambda b, s: (layer,) + (0,) * len(shape))
    in_specs = [
        pl.BlockSpec((1, t, nc), lambda b, s: (b, s, 0)),
        pl.BlockSpec((1, hal, 7 * w), lambda b, s: (b, jnp.maximum(s * hb - 1, 0), 0)),
        pl.BlockSpec((1, t, d), lambda b, s: (b, s, 0)),
        pl.BlockSpec((1, N_MOD, d), lambda b, s: (b, 0, 0)),
        lyr(p["conv_a"].shape[1], w),
        lyr(p["dw_b_w"].shape[1], w),
        cst(1, w), cst(1, w), cst(1, w), cst(1, w), cst(1, w),
        lyr(n_heads, SGU_BLOCK, SGU_BLOCK),
        cst(SGU_BLOCK, w),
        lyr(w, d), lyr(w, d), lyr(w, d),
        cst(1, 3 * d),
        lyr(d, d),
        cst(1, d), cst(1, d),
        cst(d, 2 * ROUTER_LANES), cst(1, ROUTER_LANES),
    ]
    out_specs = [
        pl.BlockSpec((1, t, d), lambda b, s: (b, s, 0)),
        pl.BlockSpec((1, t, d), lambda b, s: (b, s, 0)),
        pl.BlockSpec((1, t, ROUTER_LANES), lambda b, s: (b, s, 0)),
    ]
    out_shape = [
        jax.ShapeDtypeStruct((bsz, seq, d), F32),
        jax.ShapeDtypeStruct((bsz, seq, d), BF16),
        jax.ShapeDtypeStruct((bsz, seq, ROUTER_LANES), F32),
    ]
    n_exp = p["w_router"].shape[-1]
    wr = jnp.pad(p["w_router"][layer], ((0, 0), (0, ROUTER_LANES - n_exp)))
    wr_hi = wr.astype(BF16)
    wr = jnp.concatenate([wr_hi, (wr - wr_hi.astype(F32)).astype(BF16)], axis=1)
    br = jnp.pad(p["b_router"][layer], (0, ROUTER_LANES - n_exp)).reshape(1, -1)
    sb = jnp.repeat(p["sgu_b"][layer].T, w // n_heads, axis=1)
    return pl.pallas_call(
        functools.partial(_mixer_kernel, alpha=alpha),
        out_shape=out_shape,
        grid=grid,
        in_specs=in_specs,
        out_specs=out_specs,
        scratch_shapes=[pltpu.VMEM((t + hal, w), F32), pltpu.VMEM((t + hal, w), F32),
                        pltpu.VMEM((t, w), F32),
                        pltpu.VMEM((SUBLANES - 1, t + hal - SUBLANES, min(w, CONV_COLS)), F32)],
        compiler_params=pltpu.CompilerParams(dimension_semantics=("parallel", "parallel"),
                                             vmem_limit_bytes=VMEM_LIMIT),
        name="mixer",
    )(proj, proj, x, mod, p["conv_a"], p["dw_b_w"], row(p["dw_b_b"]),
      row(p["ln_b_g"]), row(p["ln_b_b"]), row(p["ln_c_g"]), row(p["ln_c_b"]),
      p["sgu_w"], sb, p["w_out_a"], p["w_out_b"], p["w_out_c"], row(p["b_gate"]),
      p["w_o"], row(p["ln1_g"]), row(p["ln1_b"]), wr, br)


def _moe_kernel(be_ref, nb_ref, x_ref, wgu_ref, bgu_ref, wd_ref, bd_ref, o_ref):
    f = wd_ref.shape[0]
    i = pl.program_id(0)

    @pl.when(i < nb_ref[0])
    def _():
        gu = jnp.dot(x_ref[...], wgu_ref[...], preferred_element_type=F32) + bgu_ref[0]
        gate = jnp.minimum(gu[:, :f], SWIGLU_LIMIT)
        up = jnp.clip(gu[:, f:], -SWIGLU_LIMIT, SWIGLU_LIMIT)
        act = (up + 1.0) * (gate * _sigmoid(SWIGLU_ALPHA * gate))
        y = jnp.dot(act.astype(BF16), wd_ref[...], preferred_element_type=F32) + bd_ref[0]
        o_ref[...] = y.astype(o_ref.dtype)

    @pl.when(i >= nb_ref[0])
    def _():
        o_ref[...] = jnp.zeros(o_ref.shape, o_ref.dtype)


def _moe_call(block_e, n_used, xb, w_gu, b_gu, w_down, b_down, layer, tm):
    n_rows, d = xb.shape
    n_exp, _, f2 = w_gu.shape[1:]
    f = f2 // 2
    grid_spec = pltpu.PrefetchScalarGridSpec(
        num_scalar_prefetch=2,
        grid=(n_rows // tm,),
        in_specs=[
            pl.BlockSpec((tm, d), lambda i, be, nb: (i, 0)),
            pl.BlockSpec((None, None, d, f2), lambda i, be, nb: (layer, be[i], 0, 0)),
            pl.BlockSpec((None, None, 1, f2), lambda i, be, nb: (layer, be[i], 0, 0)),
            pl.BlockSpec((None, None, f, d), lambda i, be, nb: (layer, be[i], 0, 0)),
            pl.BlockSpec((None, None, 1, d), lambda i, be, nb: (layer, be[i], 0, 0)),
        ],
        out_specs=pl.BlockSpec((tm, d), lambda i, be, nb: (i, 0)),
    )
    return pl.pallas_call(
        _moe_kernel,
        out_shape=jax.ShapeDtypeStruct((n_rows, d), BF16),
        grid_spec=grid_spec,
        compiler_params=pltpu.CompilerParams(dimension_semantics=("arbitrary",),
                                             vmem_limit_bytes=VMEM_LIMIT),
        name="moe_experts",
    )(block_e, n_used, xb, w_gu, b_gu.reshape(b_gu.shape[0], n_exp, 1, f2),
      w_down, b_down.reshape(b_down.shape[0], n_exp, 1, d))


def _ln2_kernel(x_ref, yg_ref, tw_ref, mod_ref, g_ref, b_ref, o_ref, *, alpha):
    tw = tw_ref[0]
    y = tw[:, 0:1] * yg_ref[0, 0].astype(F32)
    for k in range(1, TOP_K):
        y = y + tw[:, k:k + 1] * yg_ref[k, 0].astype(F32)
    o_ref[0] = _layer_norm(alpha * x_ref[0] + mod_ref[0, 5:6, :] * y, g_ref[...], b_ref[...])


def _ln2_call(x, yg, top_w, mod, g, b, alpha):
    bsz, seq, d = x.shape
    t = min(seq, SEQ_TILE)
    return pl.pallas_call(
        functools.partial(_ln2_kernel, alpha=alpha),
        out_shape=jax.ShapeDtypeStruct((bsz, seq, d), F32),
        grid=(bsz, seq // t),
        in_specs=[pl.BlockSpec((1, t, d), lambda b_, s: (b_, s, 0)),
                  pl.BlockSpec((TOP_K, 1, t, d), lambda b_, s: (0, b_, s, 0)),
                  pl.BlockSpec((1, t, TOP_K), lambda b_, s: (b_, s, 0)),
                  pl.BlockSpec((1, N_MOD, d), lambda b_, s: (b_, 0, 0)),
                  pl.BlockSpec((1, d), lambda b_, s: (0, 0)),
                  pl.BlockSpec((1, d), lambda b_, s: (0, 0))],
        out_specs=pl.BlockSpec((1, t, d), lambda b_, s: (b_, s, 0)),
        compiler_params=pltpu.CompilerParams(dimension_semantics=("parallel", "parallel"),
                                             vmem_limit_bytes=VMEM_LIMIT),
        name="moe_combine_ln",
    )(x, yg, top_w, mod, g.reshape(1, d), b.reshape(1, d))


def _route(logits, n_exp, tm):
    n = logits.shape[0]
    nk = n * TOP_K
    top_logit, top_idx = lax.top_k(logits[:, :n_exp], TOP_K)
    top_w = jax.nn.softmax(top_logit, axis=-1)
    onehot = (top_idx[:, :, None] == jnp.arange(n_exp)[None, None, :]).astype(jnp.int32).sum(1)
    incl = jnp.cumsum(onehot, axis=0)
    counts = incl[-1]
    rank = incl - onehot
    padded = (counts + tm - 1) // tm * tm
    pend = jnp.cumsum(padded)
    pstart = pend - padded
    dest = pstart[top_idx] + jnp.take_along_axis(rank, top_idx, axis=1)
    n_rows = (nk + tm - 1) // tm * tm + n_exp * tm
    n_blocks = n_rows // tm
    tok = jnp.broadcast_to(jnp.arange(n, dtype=jnp.int32)[:, None], (n, TOP_K))
    row_token = jnp.zeros((n_rows,), jnp.int32).at[dest.reshape(-1)].set(tok.reshape(-1))
    n_used = (pend[-1] // tm).astype(jnp.int32)
    blk = jnp.minimum(jnp.arange(n_blocks, dtype=jnp.int32), n_used - 1)
    block_e = jnp.minimum(jnp.searchsorted(pend, blk * tm, side="right"), n_exp - 1).astype(jnp.int32)
    return top_w, dest.astype(jnp.int32), row_token, block_e, n_used.reshape(1)


def kernel(x, c, w_ada, b_ada, ada_table, w_in, conv_a, dw_b_w, dw_b_b, ln_b_g, ln_b_b, ln_c_g, ln_c_b, sgu_w, sgu_b, w_out_a, w_out_b, w_out_c, w_gate, b_gate, w_o, ln1_g, ln1_b, w_router, b_router, w_gu, b_gu, w_down, b_down, ln2_g, ln2_b):
    bsz, seq, d = x.shape
    depth = w_in.shape[0]
    n_exp = w_router.shape[-1]
    w = conv_a.shape[-1]
    alpha = (2.0 * depth) ** 0.25
    n = bsz * seq

    mod_shared = _ada_call(c, w_ada, b_ada).reshape(bsz, N_MOD, d)
    w_cat = jnp.concatenate([w_in, w_gate], axis=-1).astype(BF16)
    p = dict(conv_a=conv_a, dw_b_w=dw_b_w, dw_b_b=dw_b_b, ln_b_g=ln_b_g, ln_b_b=ln_b_b,
             ln_c_g=ln_c_g, ln_c_b=ln_c_b, sgu_w=sgu_w, sgu_b=sgu_b,
             w_out_a=w_out_a.astype(BF16), w_out_b=w_out_b.astype(BF16),
             w_out_c=w_out_c.astype(BF16), b_gate=b_gate, w_o=w_o.astype(BF16),
             ln1_g=ln1_g, ln1_b=ln1_b, w_router=w_router, b_router=b_router)
    w_gu_b = _cast_call(w_gu)
    w_down_b = _cast_call(w_down)
    tm = MOE_ROWS

    for l in range(depth):
        mod = mod_shared + ada_table[l][None]
        proj = _proj_call(x, mod, w_cat, l, w)
        x, h2, logits = _mixer_call(proj, x, mod, p, l, alpha)

        top_w, dest, row_token, block_e, n_used = _route(logits.reshape(n, -1), n_exp, tm)
        xb = h2.reshape(n, d)[row_token]
        yb = _moe_call(block_e, n_used, xb, w_gu_b, b_gu, w_down_b, b_down, l, tm)
        yg = yb[dest.T].reshape(TOP_K, bsz, seq, d)
        x = _ln2_call(x, yg, top_w.reshape(bsz, seq, TOP_K), mod, ln2_g[l], ln2_b[l], alpha)
    return x
```

```python
import functools

import jax
import jax.numpy as jnp
from jax import lax
from jax.experimental import pallas as pl
from jax.experimental.pallas import tpu as pltpu

F32 = jnp.float32
BF16 = jnp.bfloat16
U32 = jnp.uint32

CHUNK = 64
SGU_BLOCK = 128
TOP_K = 4
SWIGLU_LIMIT = 7.0
SWIGLU_ALPHA = 1.702
LN_EPS = 1e-5
N_MOD = 6

LANES = 128
SUBLANES = 8
CONV_ROWS = 64
CONV_COLS = 2 * LANES
CONV_HALO = 32
ROUTER_LANES = LANES
VMEM_LIMIT = 56 * 1024 * 1024

SEQ_TILE = 256
PROJ_ROWS = 1024
MOE_ROWS = 512
ROUTER_ROWS = 512


def _layer_norm(x, g, b):
    mu = jnp.mean(x, axis=-1, keepdims=True)
    xc = x - mu
    var = jnp.mean(xc * xc, axis=-1, keepdims=True)
    return xc * lax.rsqrt(var + LN_EPS) * g + b


def _gelu(x):
    c = 0.7978845608028654
    return 0.5 * x * (1.0 + jnp.tanh(c * (x + 0.044715 * (x * x * x))))


def _sigmoid(x):
    return 1.0 / (1.0 + jnp.exp(-x))


def _pack_halves(y):
    half = y.shape[1] // 2
    hi = lax.bitcast_convert_type(y[:, :half].astype(BF16).astype(F32), U32)
    lo = lax.bitcast_convert_type(y[:, half:].astype(BF16).astype(F32), U32)
    return hi | (lo >> 16)


def _unpack_halves(p):
    hi = lax.bitcast_convert_type(p & jnp.uint32(0xFFFF0000), F32)
    lo = lax.bitcast_convert_type(p << 16, F32)
    return hi, lo


def _cast_kernel(x_ref, o_ref):
    o_ref[...] = x_ref[...].astype(o_ref.dtype)


def _cast_call(a):
    n_l, n_e, r, c = a.shape
    spec = pl.BlockSpec((None, None, r, c), lambda i, j: (i, j, 0, 0))
    return pl.pallas_call(
        _cast_kernel,
        out_shape=jax.ShapeDtypeStruct(a.shape, BF16),
        grid=(n_l, n_e),
        in_specs=[spec],
        out_specs=spec,
        compiler_params=pltpu.CompilerParams(dimension_semantics=("parallel", "parallel"),
                                             vmem_limit_bytes=VMEM_LIMIT),
        name="expert_weight_cast",
    )(a)


def _ada_kernel(c_ref, w_ref, b_ref, o_ref):
    c = c_ref[...]
    a = c * _sigmoid(c)
    o_ref[...] = jnp.dot(a, w_ref[...], preferred_element_type=F32,
                         precision=lax.Precision.HIGHEST) + b_ref[...]


def _ada_call(c, w_ada, b_ada):
    bsz, d = c.shape
    nc = w_ada.shape[1]
    bn = min(nc, 1024)
    return pl.pallas_call(
        _ada_kernel,
        out_shape=jax.ShapeDtypeStruct((bsz, nc), F32),
        grid=(nc // bn,),
        in_specs=[pl.BlockSpec((bsz, d), lambda j: (0, 0)),
                  pl.BlockSpec((d, bn), lambda j: (0, j)),
                  pl.BlockSpec((1, bn), lambda j: (0, j))],
        out_specs=pl.BlockSpec((bsz, bn), lambda j: (0, j)),
        compiler_params=pltpu.CompilerParams(dimension_semantics=("parallel",),
                                             vmem_limit_bytes=VMEM_LIMIT),
        name="ada_proj",
    )(c, w_ada, b_ada.reshape(1, nc))


def _proj_kernel(x_ref, mod_ref, w_ref, o_ref, h_ref):
    @pl.when(pl.program_id(2) == 0)
    def _():
        sh = mod_ref[0, 0:1, :]
        sc = mod_ref[0, 1:2, :]
        h_ref[...] = (x_ref[0] * (1.0 + sc) + sh).astype(BF16)

    o_ref[0] = jnp.dot(h_ref[...], w_ref[...], preferred_element_type=F32).astype(o_ref.dtype)


def _proj_call(x, mod, w_cat, layer, bn):
    bsz, seq, d = x.shape
    nc = w_cat.shape[-1]
    bm = min(seq, PROJ_ROWS)
    return pl.pallas_call(
        _proj_kernel,
        out_shape=jax.ShapeDtypeStruct((bsz, seq, nc), BF16),
        grid=(bsz, seq // bm, nc // bn),
        in_specs=[pl.BlockSpec((1, bm, d), lambda b, i, j: (b, i, 0)),
                  pl.BlockSpec((1, N_MOD, d), lambda b, i, j: (b, 0, 0)),
                  pl.BlockSpec((None, d, bn), lambda b, i, j: (layer, 0, j))],
        out_specs=pl.BlockSpec((1, bm, bn), lambda b, i, j: (b, i, j)),
        scratch_shapes=[pltpu.VMEM((bm, d), BF16)],
        compiler_params=pltpu.CompilerParams(
            dimension_semantics=("parallel", "parallel", "arbitrary"),
            vmem_limit_bytes=VMEM_LIMIT),
        name="mixer_in_proj",
    )(x, mod, w_cat)


def _mixer_kernel(proj_ref, halo_ref, x_ref, mod_ref, ca_ref, dww_ref, dwb_ref,
                  lnbg_ref, lnbb_ref, lncg_ref, lncb_ref, sw_ref, sb_ref,
                  woa_ref, wob_ref, woc_ref, bg_ref, wo_ref, ln1g_ref, ln1b_ref,
                  wr_ref, br_ref,
                  xo_ref, h2_ref, lg_ref,
                  abuf, bbuf, cbuf, sbuf, *, alpha):
    t = x_ref.shape[1]
    d = x_ref.shape[2]
    w = ca_ref.shape[1]
    n_heads = sw_ref.shape[0]
    hd = w // n_heads
    ka = ca_ref.shape[0]
    kb = dww_ref.shape[0]
    hal = CONV_HALO

    def pf(k):
        return proj_ref[0, :, k * w:(k + 1) * w].astype(F32)

    def hf(k):
        return halo_ref[0, :, k * w:(k + 1) * w].astype(F32)

    keep = jnp.where(pl.program_id(1) > 0, 1.0, 0.0).astype(F32)

    abuf[0:hal, :] = hf(0) * hf(2) * keep
    abuf[hal:hal + t, :] = pf(0) * pf(2)
    conv_a = ca_ref[0:1, :] * abuf[hal - ka + 1:hal - ka + 1 + t, :]
    for k in range(1, ka):
        conv_a = conv_a + ca_ref[k:k + 1, :] * abuf[hal - ka + 1 + k:hal - ka + 1 + k + t, :]
    y_a = (pf(1) * conv_a).astype(BF16)
    o_a = jnp.dot(y_a, woa_ref[...], preferred_element_type=F32)

    bbuf[0:hal, :] = hf(3) * _sigmoid(hf(4)) * keep
    bbuf[hal:hal + t, :] = pf(3) * _sigmoid(pf(4))
    rows = CONV_ROWS
    cols = sbuf.shape[2]
    span = sbuf.shape[1]
    for c0 in range(0, w, cols):
        for r in range(1, SUBLANES):
            sbuf[r - 1] = bbuf[r:r + span, c0:c0 + cols]
        for r0 in range(0, t, rows):
            acc = None
            for k in range(kb):
                q, r = divmod(hal - kb + 1 + k, SUBLANES)
                lo = r0 + SUBLANES * q
                if r == 0:
                    src = bbuf[lo:lo + rows, c0:c0 + cols]
                else:
                    src = sbuf[r - 1, lo:lo + rows, :]
                term = dww_ref[k:k + 1, c0:c0 + cols] * src
                acc = term if acc is None else acc + term
            cbuf[r0:r0 + rows, c0:c0 + cols] = acc
    y_b = _layer_norm(cbuf[...] + dwb_ref[...], lnbg_ref[...], lnbb_ref[...])
    y_b = (y_b * _sigmoid(y_b)).astype(BF16)
    o_b = jnp.dot(y_b, wob_ref[...], preferred_element_type=F32)

    v = _layer_norm(_gelu(pf(6)), lncg_ref[...], lncb_ref[...]).astype(BF16)
    ri = lax.broadcasted_iota(jnp.int32, (SGU_BLOCK, SGU_BLOCK), 0) // CHUNK
    ci = lax.broadcasted_iota(jnp.int32, (SGU_BLOCK, SGU_BLOCK), 1) // CHUNK
    causal = ri >= ci
    for g in range(n_heads):
        wm = jnp.where(causal, sw_ref[g], 0.0).astype(BF16)
        for blk in range(t // SGU_BLOCK):
            r0 = blk * SGU_BLOCK
            cbuf[r0:r0 + SGU_BLOCK, g * hd:(g + 1) * hd] = (
                jnp.dot(wm, v[r0:r0 + SGU_BLOCK, g * hd:(g + 1) * hd], preferred_element_type=F32)
                + sb_ref[:, g * hd:(g + 1) * hd])
    y_c = (_gelu(pf(5)) * cbuf[...]).astype(BF16)
    o_c = jnp.dot(y_c, woc_ref[...], preferred_element_type=F32)

    g0 = 7 * w

    def gate(k):
        z = proj_ref[0, :, g0 + k * d:g0 + (k + 1) * d].astype(F32) + bg_ref[:, k * d:(k + 1) * d]
        return _sigmoid(z)

    merged = (gate(0) * o_a + gate(1) * o_b + gate(2) * o_c).astype(BF16)
    mix = jnp.dot(merged, wo_ref[...], preferred_element_type=F32)
    xn = _layer_norm(alpha * x_ref[0] + mod_ref[0, 2:3, :] * mix, ln1g_ref[...], ln1b_ref[...])
    xo_ref[0] = xn

    h2 = xn * (1.0 + mod_ref[0, 4:5, :]) + mod_ref[0, 3:4, :]
    h2_hi = h2.astype(BF16)
    h2_ref[0] = h2_hi
    h2_lo = (h2 - h2_hi.astype(F32)).astype(BF16)
    p_hi = jnp.dot(h2_hi, wr_ref[...], preferred_element_type=F32)
    p_lo = jnp.dot(h2_lo, wr_ref[:, 0:ROUTER_LANES], preferred_element_type=F32)
    lg_ref[0] = p_hi[:, 0:ROUTER_LANES] + p_hi[:, ROUTER_LANES:] + p_lo + br_ref[...]


def _const_spec(shape, index):
    return pl.BlockSpec(shape, index, pipeline_mode=pl.Buffered(1))


def _mixer_call(proj, x, mod, p, layer, alpha):
    bsz, seq, d = x.shape
    w = p["conv_a"].shape[-1]
    nc = proj.shape[-1]
    t = min(seq, SEQ_TILE)
    hal = CONV_HALO
    hb = t // hal
    n_heads = p["sgu_w"].shape[1]
    row = lambda a: a[layer].reshape(1, -1)
    grid = (bsz, seq // t)
    cst = lambda *shape: _const_spec(shape, lambda b, s: (0,) * len(shape))
    lyr = lambda *shape: _const_spec((None,) + shape, lambda b, s: (layer,) + (0,) * len(shape))
    in_specs = [
        pl.BlockSpec((1, t, nc), lambda b, s: (b, s, 0)),
        pl.BlockSpec((1, hal, 7 * w), lambda b, s: (b, jnp.maximum(s * hb - 1, 0), 0)),
        pl.BlockSpec((1, t, d), lambda b, s: (b, s, 0)),
        pl.BlockSpec((1, N_MOD, d), lambda b, s: (b, 0, 0)),
        lyr(p["conv_a"].shape[1], w),
        lyr(p["dw_b_w"].shape[1], w),
        cst(1, w), cst(1, w), cst(1, w), cst(1, w), cst(1, w),
        lyr(n_heads, SGU_BLOCK, SGU_BLOCK),
        cst(SGU_BLOCK, w),
        lyr(w, d), lyr(w, d), lyr(w, d),
        cst(1, 3 * d),
        lyr(d, d),
        cst(1, d), cst(1, d),
        cst(d, 2 * ROUTER_LANES), cst(1, ROUTER_LANES),
    ]
    out_specs = [
        pl.BlockSpec((1, t, d), lambda b, s: (b, s, 0)),
        pl.BlockSpec((1, t, d), lambda b, s: (b, s, 0)),
        pl.BlockSpec((1, t, ROUTER_LANES), lambda b, s: (b, s, 0)),
    ]
    out_shape = [
        jax.ShapeDtypeStruct((bsz, seq, d), F32),
        jax.ShapeDtypeStruct((bsz, seq, d), BF16),
        jax.ShapeDtypeStruct((bsz, seq, ROUTER_LANES), F32),
    ]
    n_exp = p["w_router"].shape[-1]
    wr = jnp.pad(p["w_router"][layer], ((0, 0), (0, ROUTER_LANES - n_exp)))
    wr_hi = wr.astype(BF16)
    wr = jnp.concatenate([wr_hi, (wr - wr_hi.astype(F32)).astype(BF16)], axis=1)
    br = jnp.pad(p["b_router"][layer], (0, ROUTER_LANES - n_exp)).reshape(1, -1)
    sb = jnp.repeat(p["sgu_b"][layer].T, w // n_heads, axis=1)
    return pl.pallas_call(
        functools.partial(_mixer_kernel, alpha=alpha),
        out_shape=out_shape,
        grid=grid,
        in_specs=in_specs,
        out_specs=out_specs,
        scratch_shapes=[pltpu.VMEM((t + hal, w), F32), pltpu.VMEM((t + hal, w), F32),
                        pltpu.VMEM((t, w), F32),
                        pltpu.VMEM((SUBLANES - 1, t + hal - SUBLANES, min(w, CONV_COLS)), F32)],
        compiler_params=pltpu.CompilerParams(dimension_semantics=("parallel", "parallel"),
                                             vmem_limit_bytes=VMEM_LIMIT),
        name="mixer",
    )(proj, proj, x, mod, p["conv_a"], p["dw_b_w"], row(p["dw_b_b"]),
      row(p["ln_b_g"]), row(p["ln_b_b"]), row(p["ln_c_g"]), row(p["ln_c_b"]),
      p["sgu_w"], sb, p["w_out_a"], p["w_out_b"], p["w_out_c"], row(p["b_gate"]),
      p["w_o"], row(p["ln1_g"]), row(p["ln1_b"]), wr, br)


def _moe_kernel(be_ref, nb_ref, x_ref, wgu_ref, bgu_ref, wd_ref, bd_ref, o_ref):
    f = wd_ref.shape[0]
    i = pl.program_id(0)

    @pl.when(i < nb_ref[0])
    def _():
        gu = jnp.dot(x_ref[...], wgu_ref[...], preferred_element_type=F32) + bgu_ref[0]
        gate = jnp.minimum(gu[:, :f], SWIGLU_LIMIT)
        up = jnp.clip(gu[:, f:], -SWIGLU_LIMIT, SWIGLU_LIMIT)
        act = (up + 1.0) * (gate * _sigmoid(SWIGLU_ALPHA * gate))
        y = jnp.dot(act.astype(BF16), wd_ref[...], preferred_element_type=F32) + bd_ref[0]
        o_ref[...] = _pack_halves(y)

    @pl.when(i >= nb_ref[0])
    def _():
        o_ref[...] = jnp.zeros(o_ref.shape, o_ref.dtype)


def _moe_call(block_e, n_used, xb, w_gu, b_gu, w_down, b_down, layer, tm):
    n_rows, d = xb.shape
    n_exp, _, f2 = w_gu.shape[1:]
    f = f2 // 2
    grid_spec = pltpu.PrefetchScalarGridSpec(
        num_scalar_prefetch=2,
        grid=(n_rows // tm,),
        in_specs=[
            pl.BlockSpec((tm, d), lambda i, be, nb: (i, 0)),
            pl.BlockSpec((None, None, d, f2), lambda i, be, nb: (layer, be[i], 0, 0)),
            pl.BlockSpec((None, None, 1, f2), lambda i, be, nb: (layer, be[i], 0, 0)),
            pl.BlockSpec((None, None, f, d), lambda i, be, nb: (layer, be[i], 0, 0)),
            pl.BlockSpec((None, None, 1, d), lambda i, be, nb: (layer, be[i], 0, 0)),
        ],
        out_specs=pl.BlockSpec((tm, d // 2), lambda i, be, nb: (i, 0)),
    )
    return pl.pallas_call(
        _moe_kernel,
        out_shape=jax.ShapeDtypeStruct((n_rows, d // 2), U32),
        grid_spec=grid_spec,
        compiler_params=pltpu.CompilerParams(dimension_semantics=("arbitrary",),
                                             vmem_limit_bytes=VMEM_LIMIT),
        name="moe_experts",
    )(block_e, n_used, xb, w_gu, b_gu.reshape(b_gu.shape[0], n_exp, 1, f2),
      w_down, b_down.reshape(b_down.shape[0], n_exp, 1, d))


def _ln2_kernel(dcur_ref, dnxt_ref, x_ref, yb_ref, tw_ref, mod_ref, g_ref, b_ref, o_ref,
                buf, sem, *, alpha):
    t = x_ref.shape[1]
    n_rows_step = TOP_K * t
    i = pl.program_id(0)
    slot = lax.rem(i, 2)

    def row_copy(d_ref, r, s):
        return pltpu.make_async_copy(yb_ref.at[pl.ds(d_ref[0, 0, r], 1)],
                                     buf.at[s, pl.ds(r, 1)], sem.at[s])

    def start_gather(d_ref, s):
        for r in range(n_rows_step):
            row_copy(d_ref, r, s).start()

    @pl.when(i == 0)
    def _():
        start_gather(dcur_ref, 0)

    @pl.when(i + 1 < pl.num_programs(0))
    def _():
        start_gather(dnxt_ref, 1 - slot)

    pltpu.make_async_copy(yb_ref.at[pl.ds(0, n_rows_step)], buf.at[slot], sem.at[slot]).wait()

    tw = tw_ref[0]
    y_hi = y_lo = None
    for k in range(TOP_K):
        hi, lo = _unpack_halves(buf[slot, k * t:(k + 1) * t, :])
        wk = tw[:, k:k + 1]
        y_hi = wk * hi if y_hi is None else y_hi + wk * hi
        y_lo = wk * lo if y_lo is None else y_lo + wk * lo
    y = jnp.concatenate([y_hi, y_lo], axis=1)
    o_ref[0] = _layer_norm(alpha * x_ref[0] + mod_ref[0, 5:6, :] * y, g_ref[...], b_ref[...])


def _ln2_call(x, yb, dest, top_w, mod, g, b, alpha):
    bsz, seq, d = x.shape
    t = min(seq, SEQ_TILE)
    spb = seq // t
    nst = bsz * spb
    dsteps = dest.reshape(nst, t, TOP_K).transpose(0, 2, 1).reshape(nst, 1, TOP_K * t)
    smem = lambda im: pl.BlockSpec((1, 1, TOP_K * t), im, memory_space=pltpu.SMEM)
    tile = lambda last: pl.BlockSpec((1, t, last), lambda i: (i // spb, i % spb, 0))
    return pl.pallas_call(
        functools.partial(_ln2_kernel, alpha=alpha),
        out_shape=jax.ShapeDtypeStruct((bsz, seq, d), F32),
        grid=(nst,),
        in_specs=[smem(lambda i: (i, 0, 0)),
                  smem(lambda i: (jnp.minimum(i + 1, nst - 1), 0, 0)),
                  tile(d),
                  pl.BlockSpec(memory_space=pl.ANY),
                  tile(TOP_K),
                  pl.BlockSpec((1, N_MOD, d), lambda i: (i // spb, 0, 0)),
                  pl.BlockSpec((1, d), lambda i: (0, 0)),
                  pl.BlockSpec((1, d), lambda i: (0, 0))],
        out_specs=tile(d),
        scratch_shapes=[pltpu.VMEM((2, TOP_K * t, d // 2), U32), pltpu.SemaphoreType.DMA((2,))],
        compiler_params=pltpu.CompilerParams(dimension_semantics=("arbitrary",),
                                             vmem_limit_bytes=VMEM_LIMIT),
        name="moe_combine_ln",
    )(dsteps, dsteps, x, yb, top_w, mod, g.reshape(1, d), b.reshape(1, d))


def _router_kernel(lg_ref, o_ref, cnt_ref, carry, *, n_exp):
    tr = lg_ref.shape[0]

    @pl.when(pl.program_id(0) == 0)
    def _():
        carry[...] = jnp.zeros(carry.shape, carry.dtype)

    lane = lax.broadcasted_iota(jnp.int32, (tr, LANES), 1)
    lane_f = lane.astype(F32)
    neg = jnp.float32(-jnp.inf)
    lg = jnp.where(lane < n_exp, lg_ref[...], neg)
    sels, tops, ids = [], [], []
    for _ in range(TOP_K):
        m = jnp.max(lg, axis=-1, keepdims=True)
        idx = jnp.min(jnp.where(lg == m, lane_f, float(LANES)), axis=-1, keepdims=True)
        sel = lane_f == idx
        lg = jnp.where(sel, neg, lg)
        sels.append(sel)
        tops.append(m)
        ids.append(idx)
    exps = [jnp.exp(m - tops[0]) for m in tops]
    denom = exps[0]
    for e in exps[1:]:
        denom = denom + e
    onehot = jnp.where(sels[0], 1.0, 0.0)
    for sel in sels[1:]:
        onehot = onehot + jnp.where(sel, 1.0, 0.0)
    rr = lax.broadcasted_iota(jnp.int32, (tr, tr), 0)
    cc = lax.broadcasted_iota(jnp.int32, (tr, tr), 1)
    earlier = jnp.where(cc < rr, 1.0, 0.0).astype(BF16)
    prefix = jnp.dot(earlier, onehot.astype(BF16), preferred_element_type=F32) + carry[0:1, :]
    out = jnp.zeros((tr, LANES), F32)
    for k in range(TOP_K):
        rank = jnp.sum(jnp.where(sels[k], prefix, 0.0), axis=-1, keepdims=True)
        out = jnp.where(lane == k, exps[k] / denom, out)
        out = jnp.where(lane == TOP_K + k, ids[k], out)
        out = jnp.where(lane == 2 * TOP_K + k, rank, out)
    o_ref[...] = out
    carry[0:1, :] = carry[0:1, :] + jnp.sum(onehot, axis=0, keepdims=True)
    cnt_ref[...] = carry[...]


def _router_call(logits, n_exp):
    n = logits.shape[0]
    tr = min(n, ROUTER_ROWS)
    return pl.pallas_call(
        functools.partial(_router_kernel, n_exp=n_exp),
        out_shape=[jax.ShapeDtypeStruct((n, LANES), F32),
                   jax.ShapeDtypeStruct((SUBLANES, LANES), F32)],
        grid=(n // tr,),
        in_specs=[pl.BlockSpec((tr, LANES), lambda i: (i, 0))],
        out_specs=[pl.BlockSpec((tr, LANES), lambda i: (i, 0)),
                   pl.BlockSpec((SUBLANES, LANES), lambda i: (0, 0))],
        scratch_shapes=[pltpu.VMEM((SUBLANES, LANES), F32)],
        compiler_params=pltpu.CompilerParams(dimension_semantics=("arbitrary",),
                                             vmem_limit_bytes=VMEM_LIMIT),
        name="router_topk",
    )(logits)


def _route(logits, n_exp, tm):
    n = logits.shape[0]
    nk = n * TOP_K
    routed, cnt = _router_call(logits, n_exp)
    top_w = routed[:, 0:TOP_K]
    top_idx = routed[:, TOP_K:2 * TOP_K].astype(jnp.int32)
    rank = routed[:, 2 * TOP_K:3 * TOP_K].astype(jnp.int32)
    counts = cnt[0, :n_exp].astype(jnp.int32)
    padded = (counts + tm - 1) // tm * tm
    pend = jnp.cumsum(padded)
    pstart = pend - padded
    dest = pstart[top_idx] + rank
    n_rows = (nk + tm - 1) // tm * tm + n_exp * tm
    n_blocks = n_rows // tm
    tok = jnp.broadcast_to(jnp.arange(n, dtype=jnp.int32)[:, None], (n, TOP_K))
    row_token = jnp.zeros((n_rows,), jnp.int32).at[dest.reshape(-1)].set(tok.reshape(-1))
    n_used = (pend[-1] // tm).astype(jnp.int32)
    blk = jnp.minimum(jnp.arange(n_blocks, dtype=jnp.int32), n_used - 1)
    block_e = jnp.minimum(jnp.searchsorted(pend, blk * tm, side="right"), n_exp - 1).astype(jnp.int32)
    return top_w, dest.astype(jnp.int32), row_token, block_e, n_used.reshape(1)


def kernel(x, c, w_ada, b_ada, ada_table, w_in, conv_a, dw_b_w, dw_b_b, ln_b_g, ln_b_b, ln_c_g, ln_c_b, sgu_w, sgu_b, w_out_a, w_out_b, w_out_c, w_gate, b_gate, w_o, ln1_g, ln1_b, w_router, b_router, w_gu, b_gu, w_down, b_down, ln2_g, ln2_b):
    bsz, seq, d = x.shape
    depth = w_in.shape[0]
    n_exp = w_router.shape[-1]
    w = conv_a.shape[-1]
    alpha = (2.0 * depth) ** 0.25
    n = bsz * seq

    mod_shared = _ada_call(c, w_ada, b_ada).reshape(bsz, N_MOD, d)
    w_cat = jnp.concatenate([w_in, w_gate], axis=-1).astype(BF16)
    p = dict(conv_a=conv_a, dw_b_w=dw_b_w, dw_b_b=dw_b_b, ln_b_g=ln_b_g, ln_b_b=ln_b_b,
             ln_c_g=ln_c_g, ln_c_b=ln_c_b, sgu_w=sgu_w, sgu_b=sgu_b,
             w_out_a=w_out_a.astype(BF16), w_out_b=w_out_b.astype(BF16),
             w_out_c=w_out_c.astype(BF16), b_gate=b_gate, w_o=w_o.astype(BF16),
             ln1_g=ln1_g, ln1_b=ln1_b, w_router=w_router, b_router=b_router)
    w_gu_b = _cast_call(w_gu)
    w_down_b = _cast_call(w_down)
    tm = MOE_ROWS

    for l in range(depth):
        mod = mod_shared + ada_table[l][None]
        proj = _proj_call(x, mod, w_cat, l, w)
        x, h2, logits = _mixer_call(proj, x, mod, p, l, alpha)

        top_w, dest, row_token, block_e, n_used = _route(logits.reshape(n, -1), n_exp, tm)
        xb = h2.reshape(n, d)[row_token]
        yb = _moe_call(block_e, n_used, xb, w_gu_b, b_gu, w_down_b, b_down, l, tm)
        x = _ln2_call(x, yb, dest, top_w.reshape(bsz, seq, TOP_K), mod, ln2_g[l], ln2_b[l], alpha)
    return x
```

```python
import functools

import jax
import jax.numpy as jnp
from jax import lax
from jax.experimental import pallas as pl
from jax.experimental.pallas import tpu as pltpu

F32 = jnp.float32
BF16 = jnp.bfloat16
U32 = jnp.uint32

CHUNK = 64
SGU_BLOCK = 128
TOP_K = 4
SWIGLU_LIMIT = 7.0
SWIGLU_ALPHA = 1.702
LN_EPS = 1e-5
N_MOD = 6

LANES = 128
SUBLANES = 8
CONV_ROWS = 64
CONV_COLS = 2 * LANES
CONV_HALO = 32
ROUTER_LANES = LANES
VMEM_LIMIT = 56 * 1024 * 1024

SEQ_TILE = 256
PROJ_ROWS = 1024
MOE_ROWS = 512
ROUTER_ROWS = 512


def _layer_norm(x, g, b):
    mu = jnp.mean(x, axis=-1, keepdims=True)
    xc = x - mu
    var = jnp.mean(xc * xc, axis=-1, keepdims=True)
    return xc * lax.rsqrt(var + LN_EPS) * g + b


def _gelu(x):
    c = 0.7978845608028654
    return 0.5 * x * (1.0 + jnp.tanh(c * (x + 0.044715 * (x * x * x))))


def _sigmoid(x):
    return 1.0 / (1.0 + jnp.exp(-x))


def _pack_halves(y):
    half = y.shape[1] // 2
    hi = lax.bitcast_convert_type(y[:, :half].astype(BF16).astype(F32), U32)
    lo = lax.bitcast_convert_type(y[:, half:].astype(BF16).astype(F32), U32)
    return hi | (lo >> 16)


def _unpack_halves(p):
    hi = lax.bitcast_convert_type(p & jnp.uint32(0xFFFF0000), F32)
    lo = lax.bitcast_convert_type(p << 16, F32)
    return hi, lo


def _ada_kernel(c_ref, w_ref, b_ref, o_ref):
    c = c_ref[...]
    a = c * _sigmoid(c)
    o_ref[...] = jnp.dot(a, w_ref[...], preferred_element_type=F32,
                         precision=lax.Precision.HIGHEST) + b_ref[...]


def _ada_call(c, w_ada, b_ada):
    bsz, d = c.shape
    nc = w_ada.shape[1]
    bn = min(nc, 1024)
    return pl.pallas_call(
        _ada_kernel,
        out_shape=jax.ShapeDtypeStruct((bsz, nc), F32),
        grid=(nc // bn,),
        in_specs=[pl.BlockSpec((bsz, d), lambda j: (0, 0)),
                  pl.BlockSpec((d, bn), lambda j: (0, j)),
                  pl.BlockSpec((1, bn), lambda j: (0, j))],
        out_specs=pl.BlockSpec((bsz, bn), lambda j: (0, j)),
        compiler_params=pltpu.CompilerParams(dimension_semantics=("parallel",),
                                             vmem_limit_bytes=VMEM_LIMIT),
        name="ada_proj",
    )(c, w_ada, b_ada.reshape(1, nc))


def _proj_kernel(x_ref, mod_ref, w_ref, o_ref, h_ref):
    @pl.when(pl.program_id(2) == 0)
    def _():
        sh = mod_ref[0, 0:1, :]
        sc = mod_ref[0, 1:2, :]
        h_ref[...] = (x_ref[0] * (1.0 + sc) + sh).astype(BF16)

    o_ref[0] = jnp.dot(h_ref[...], w_ref[...], preferred_element_type=F32).astype(o_ref.dtype)


def _proj_call(x, mod, w_cat, layer, bn):
    bsz, seq, d = x.shape
    nc = w_cat.shape[-1]
    bm = min(seq, PROJ_ROWS)
    return pl.pallas_call(
        _proj_kernel,
        out_shape=jax.ShapeDtypeStruct((bsz, seq, nc), BF16),
        grid=(bsz, seq // bm, nc // bn),
        in_specs=[pl.BlockSpec((1, bm, d), lambda b, i, j: (b, i, 0)),
                  pl.BlockSpec((1, N_MOD, d), lambda b, i, j: (b, 0, 0)),
                  pl.BlockSpec((None, d, bn), lambda b, i, j: (layer, 0, j))],
        out_specs=pl.BlockSpec((1, bm, bn), lambda b, i, j: (b, i, j)),
        scratch_shapes=[pltpu.VMEM((bm, d), BF16)],
        compiler_params=pltpu.CompilerParams(
            dimension_semantics=("parallel", "parallel", "arbitrary"),
            vmem_limit_bytes=VMEM_LIMIT),
        name="mixer_in_proj",
    )(x, mod, w_cat)


def _mixer_kernel(proj_ref, halo_ref, x_ref, mod_ref, ca_ref, dww_ref, dwb_ref,
                  lnbg_ref, lnbb_ref, lncg_ref, lncb_ref, sw_ref, sb_ref,
                  woa_ref, wob_ref, woc_ref, bg_ref, wo_ref, ln1g_ref, ln1b_ref,
                  wr_ref, br_ref,
                  xo_ref, h2_ref, lg_ref,
                  abuf, bbuf, cbuf, sbuf, *, alpha):
    t = x_ref.shape[1]
    d = x_ref.shape[2]
    w = ca_ref.shape[1]
    n_heads = sw_ref.shape[0]
    hd = w // n_heads
    ka = ca_ref.shape[0]
    kb = dww_ref.shape[0]
    hal = CONV_HALO

    def pf(k):
        return proj_ref[0, :, k * w:(k + 1) * w].astype(F32)

    def hf(k):
        return halo_ref[0, :, k * w:(k + 1) * w].astype(F32)

    keep = jnp.where(pl.program_id(1) > 0, 1.0, 0.0).astype(F32)

    abuf[0:hal, :] = hf(0) * hf(2) * keep
    abuf[hal:hal + t, :] = pf(0) * pf(2)
    conv_a = ca_ref[0:1, :] * abuf[hal - ka + 1:hal - ka + 1 + t, :]
    for k in range(1, ka):
        conv_a = conv_a + ca_ref[k:k + 1, :] * abuf[hal - ka + 1 + k:hal - ka + 1 + k + t, :]
    y_a = (pf(1) * conv_a).astype(BF16)
    o_a = jnp.dot(y_a, woa_ref[...], preferred_element_type=F32)

    bbuf[0:hal, :] = hf(3) * _sigmoid(hf(4)) * keep
    bbuf[hal:hal + t, :] = pf(3) * _sigmoid(pf(4))
    rows = CONV_ROWS
    cols = sbuf.shape[2]
    span = sbuf.shape[1]
    for c0 in range(0, w, cols):
        for r in range(1, SUBLANES):
            sbuf[r - 1] = bbuf[r:r + span, c0:c0 + cols]
        for r0 in range(0, t, rows):
            acc = None
            for k in range(kb):
                q, r = divmod(hal - kb + 1 + k, SUBLANES)
                lo = r0 + SUBLANES * q
                if r == 0:
                    src = bbuf[lo:lo + rows, c0:c0 + cols]
                else:
                    src = sbuf[r - 1, lo:lo + rows, :]
                term = dww_ref[k:k + 1, c0:c0 + cols] * src
                acc = term if acc is None else acc + term
            cbuf[r0:r0 + rows, c0:c0 + cols] = acc
    y_b = _layer_norm(cbuf[...] + dwb_ref[...], lnbg_ref[...], lnbb_ref[...])
    y_b = (y_b * _sigmoid(y_b)).astype(BF16)
    o_b = jnp.dot(y_b, wob_ref[...], preferred_element_type=F32)

    v = _layer_norm(_gelu(pf(6)), lncg_ref[...], lncb_ref[...]).astype(BF16)
    ri = lax.broadcasted_iota(jnp.int32, (SGU_BLOCK, SGU_BLOCK), 0) // CHUNK
    ci = lax.broadcasted_iota(jnp.int32, (SGU_BLOCK, SGU_BLOCK), 1) // CHUNK
    causal = ri >= ci
    for g in range(n_heads):
        wm = jnp.where(causal, sw_ref[g], 0.0).astype(BF16)
        for blk in range(t // SGU_BLOCK):
            r0 = blk * SGU_BLOCK
            cbuf[r0:r0 + SGU_BLOCK, g * hd:(g + 1) * hd] = (
                jnp.dot(wm, v[r0:r0 + SGU_BLOCK, g * hd:(g + 1) * hd], preferred_element_type=F32)
                + sb_ref[:, g * hd:(g + 1) * hd])
    y_c = (_gelu(pf(5)) * cbuf[...]).astype(BF16)
    o_c = jnp.dot(y_c, woc_ref[...], preferred_element_type=F32)

    g0 = 7 * w

    def gate(k):
        z = proj_ref[0, :, g0 + k * d:g0 + (k + 1) * d].astype(F32) + bg_ref[:, k * d:(k + 1) * d]
        return _sigmoid(z)

    merged = (gate(0) * o_a + gate(1) * o_b + gate(2) * o_c).astype(BF16)
    mix = jnp.dot(merged, wo_ref[...], preferred_element_type=F32)
    xn = _layer_norm(alpha * x_ref[0] + mod_ref[0, 2:3, :] * mix, ln1g_ref[...], ln1b_ref[...])
    xo_ref[0] = xn

    h2 = xn * (1.0 + mod_ref[0, 4:5, :]) + mod_ref[0, 3:4, :]
    h2_hi = h2.astype(BF16)
    h2_ref[0] = h2_hi
    h2_lo = (h2 - h2_hi.astype(F32)).astype(BF16)
    p_hi = jnp.dot(h2_hi, wr_ref[...], preferred_element_type=F32)
    p_lo = jnp.dot(h2_lo, wr_ref[:, 0:ROUTER_LANES], preferred_element_type=F32)
    lg_ref[0] = p_hi[:, 0:ROUTER_LANES] + p_hi[:, ROUTER_LANES:] + p_lo + br_ref[...]


def _const_spec(shape, index):
    return pl.BlockSpec(shape, index, pipeline_mode=pl.Buffered(1))


def _mixer_call(proj, x, mod, p, layer, alpha):
    bsz, seq, d = x.shape
    w = p["conv_a"].shape[-1]
    nc = proj.shape[-1]
    t = min(seq, SEQ_TILE)
    hal = CONV_HALO
    hb = t // hal
    n_heads = p["sgu_w"].shape[1]
    row = lambda a: a[layer].reshape(1, -1)
    grid = (bsz, seq // t)
    cst = lambda *shape: _const_spec(shape, lambda b, s: (0,) * len(shape))
    lyr = lambda *shape: _const_spec((None,) + shape, lambda b, s: (layer,) + (0,) * len(shape))
    in_specs = [
        pl.BlockSpec((1, t, nc), lambda b, s: (b, s, 0)),
        pl.BlockSpec((1, hal, 7 * w), lambda b, s: (b, jnp.maximum(s * hb - 1, 0), 0)),
        pl.BlockSpec((1, t, d), lambda b, s: (b, s, 0)),
        pl.BlockSpec((1, N_MOD, d), lambda b, s: (b, 0, 0)),
        lyr(p["conv_a"].shape[1], w),
        lyr(p["dw_b_w"].shape[1], w),
        cst(1, w), cst(1, w), cst(1, w), cst(1, w), cst(1, w),
        lyr(n_heads, SGU_BLOCK, SGU_BLOCK),
        cst(SGU_BLOCK, w),
        lyr(w, d), lyr(w, d), lyr(w, d),
        cst(1, 3 * d),
        lyr(d, d),
        cst(1, d), cst(1, d),
        cst(d, 2 * ROUTER_LANES), cst(1, ROUTER_LANES),
    ]
    out_specs = [
        pl.BlockSpec((1, t, d), lambda b, s: (b, s, 0)),
        pl.BlockSpec((1, t, d), lambda b, s: (b, s, 0)),
        pl.BlockSpec((1, t, ROUTER_LANES), lambda b, s: (b, s, 0)),
    ]
    out_shape = [
        jax.ShapeDtypeStruct((bsz, seq, d), F32),
        jax.ShapeDtypeStruct((bsz, seq, d), BF16),
        jax.ShapeDtypeStruct((bsz, seq, ROUTER_LANES), F32),
    ]
    n_exp = p["w_router"].shape[-1]
    wr = jnp.pad(p["w_router"][layer], ((0, 0), (0, ROUTER_LANES - n_exp)))
    wr_hi = wr.astype(BF16)
    wr = jnp.concatenate([wr_hi, (wr - wr_hi.astype(F32)).astype(BF16)], axis=1)
    br = jnp.pad(p["b_router"][layer], (0, ROUTER_LANES - n_exp)).reshape(1, -1)
    sb = jnp.repeat(p["sgu_b"][layer].T, w // n_heads, axis=1)
    return pl.pallas_call(
        functools.partial(_mixer_kernel, alpha=alpha),
        out_shape=out_shape,
        grid=grid,
        in_specs=in_specs,
        out_specs=out_specs,
        scratch_shapes=[pltpu.VMEM((t + hal, w), F32), pltpu.VMEM((t + hal, w), F32),
                        pltpu.VMEM((t, w), F32),
                        pltpu.VMEM((SUBLANES - 1, t + hal - SUBLANES, min(w, CONV_COLS)), F32)],
        compiler_params=pltpu.CompilerParams(dimension_semantics=("parallel", "parallel"),
                                             vmem_limit_bytes=VMEM_LIMIT),
        name="mixer",
    )(proj, proj, x, mod, p["conv_a"], p["dw_b_w"], row(p["dw_b_b"]),
      row(p["ln_b_g"]), row(p["ln_b_b"]), row(p["ln_c_g"]), row(p["ln_c_b"]),
      p["sgu_w"], sb, p["w_out_a"], p["w_out_b"], p["w_out_c"], row(p["b_gate"]),
      p["w_o"], row(p["ln1_g"]), row(p["ln1_b"]), wr, br)


def _moe_kernel(be_ref, first_ref, nxt_ref, nb_ref, x_ref, wgu_hbm, bgu_ref, wd_hbm, bd_ref, o_ref,
                gu_stage, d_stage, wgu_bf, wd_bf, sem, *, layer):
    f = wd_bf.shape[0]
    i = pl.program_id(0)

    def weight_copies(expert):
        return (pltpu.make_async_copy(wgu_hbm.at[layer, expert], gu_stage, sem.at[0]),
                pltpu.make_async_copy(wd_hbm.at[layer, expert], d_stage, sem.at[1]))

    @pl.when(i == 0)
    def _():
        for cp in weight_copies(be_ref[0]):
            cp.start()

    @pl.when(first_ref[i] == 1)
    def _():
        for cp in weight_copies(be_ref[i]):
            cp.wait()
        wgu_bf[...] = gu_stage[...].astype(BF16)
        wd_bf[...] = d_stage[...].astype(BF16)

        @pl.when(nxt_ref[i] >= 0)
        def _():
            for cp in weight_copies(nxt_ref[i]):
                cp.start()

    @pl.when(i < nb_ref[0])
    def _():
        gu = jnp.dot(x_ref[...], wgu_bf[...], preferred_element_type=F32) + bgu_ref[...]
        gate = jnp.minimum(gu[:, :f], SWIGLU_LIMIT)
        up = jnp.clip(gu[:, f:], -SWIGLU_LIMIT, SWIGLU_LIMIT)
        act = (up + 1.0) * (gate * _sigmoid(SWIGLU_ALPHA * gate))
        y = jnp.dot(act.astype(BF16), wd_bf[...], preferred_element_type=F32) + bd_ref[...]
        o_ref[...] = _pack_halves(y)

    @pl.when(i >= nb_ref[0])
    def _():
        o_ref[...] = jnp.zeros(o_ref.shape, o_ref.dtype)


def _moe_call(plan, xb, w_gu, b_gu, w_down, b_down, layer, tm):
    n_rows, d = xb.shape
    n_exp, _, f2 = w_gu.shape[1:]
    f = f2 // 2
    grid_spec = pltpu.PrefetchScalarGridSpec(
        num_scalar_prefetch=4,
        grid=(n_rows // tm,),
        in_specs=[
            pl.BlockSpec((tm, d), lambda i, *_: (i, 0)),
            pl.BlockSpec(memory_space=pl.ANY),
            pl.BlockSpec((None, None, 1, f2), lambda i, be, *_: (layer, be[i], 0, 0)),
            pl.BlockSpec(memory_space=pl.ANY),
            pl.BlockSpec((None, None, 1, d), lambda i, be, *_: (layer, be[i], 0, 0)),
        ],
        out_specs=pl.BlockSpec((tm, d // 2), lambda i, *_: (i, 0)),
        scratch_shapes=[pltpu.VMEM((d, f2), F32), pltpu.VMEM((f, d), F32),
                        pltpu.VMEM((d, f2), BF16), pltpu.VMEM((f, d), BF16),
                        pltpu.SemaphoreType.DMA((2,))],
    )
    return pl.pallas_call(
        functools.partial(_moe_kernel, layer=layer),
        out_shape=jax.ShapeDtypeStruct((n_rows, d // 2), U32),
        grid_spec=grid_spec,
        compiler_params=pltpu.CompilerParams(dimension_semantics=("arbitrary",),
                                             vmem_limit_bytes=VMEM_LIMIT),
        name="moe_experts",
    )(*plan, xb, w_gu, b_gu.reshape(b_gu.shape[0], n_exp, 1, f2),
      w_down, b_down.reshape(b_down.shape[0], n_exp, 1, d))


def _ln2_kernel(dcur_ref, dnxt_ref, x_ref, yb_ref, tw_ref, mod_ref, g_ref, b_ref, o_ref,
                buf, sem, *, alpha):
    t = x_ref.shape[1]
    n_rows_step = TOP_K * t
    i = pl.program_id(0)
    slot = lax.rem(i, 2)

    def row_copy(d_ref, r, s):
        return pltpu.make_async_copy(yb_ref.at[pl.ds(d_ref[0, 0, r], 1)],
                                     buf.at[s, pl.ds(r, 1)], sem.at[s])

    def start_gather(d_ref, s):
        for r in range(n_rows_step):
            row_copy(d_ref, r, s).start()

    @pl.when(i == 0)
    def _():
        start_gather(dcur_ref, 0)

    @pl.when(i + 1 < pl.num_programs(0))
    def _():
        start_gather(dnxt_ref, 1 - slot)

    pltpu.make_async_copy(yb_ref.at[pl.ds(0, n_rows_step)], buf.at[slot], sem.at[slot]).wait()

    tw = tw_ref[0]
    y_hi = y_lo = None
    for k in range(TOP_K):
        hi, lo = _unpack_halves(buf[slot, k * t:(k + 1) * t, :])
        wk = tw[:, k:k + 1]
        y_hi = wk * hi if y_hi is None else y_hi + wk * hi
        y_lo = wk * lo if y_lo is None else y_lo + wk * lo
    y = jnp.concatenate([y_hi, y_lo], axis=1)
    o_ref[0] = _layer_norm(alpha * x_ref[0] + mod_ref[0, 5:6, :] * y, g_ref[...], b_ref[...])


def _ln2_call(x, yb, dest, top_w, mod, g, b, alpha):
    bsz, seq, d = x.shape
    t = min(seq, SEQ_TILE)
    spb = seq // t
    nst = bsz * spb
    dsteps = dest.reshape(nst, t, TOP_K).transpose(0, 2, 1).reshape(nst, 1, TOP_K * t)
    smem = lambda im: pl.BlockSpec((1, 1, TOP_K * t), im, memory_space=pltpu.SMEM)
    tile = lambda last: pl.BlockSpec((1, t, last), lambda i: (i // spb, i % spb, 0))
    return pl.pallas_call(
        functools.partial(_ln2_kernel, alpha=alpha),
        out_shape=jax.ShapeDtypeStruct((bsz, seq, d), F32),
        grid=(nst,),
        in_specs=[smem(lambda i: (i, 0, 0)),
                  smem(lambda i: (jnp.minimum(i + 1, nst - 1), 0, 0)),
                  tile(d),
                  pl.BlockSpec(memory_space=pl.ANY),
                  tile(TOP_K),
                  pl.BlockSpec((1, N_MOD, d), lambda i: (i // spb, 0, 0)),
                  pl.BlockSpec((1, d), lambda i: (0, 0)),
                  pl.BlockSpec((1, d), lambda i: (0, 0))],
        out_specs=tile(d),
        scratch_shapes=[pltpu.VMEM((2, TOP_K * t, d // 2), U32), pltpu.SemaphoreType.DMA((2,))],
        compiler_params=pltpu.CompilerParams(dimension_semantics=("arbitrary",),
                                             vmem_limit_bytes=VMEM_LIMIT),
        name="moe_combine_ln",
    )(dsteps, dsteps, x, yb, top_w, mod, g.reshape(1, d), b.reshape(1, d))


def _router_kernel(lg_ref, o_ref, cnt_ref, carry, *, n_exp):
    tr = lg_ref.shape[0]

    @pl.when(pl.program_id(0) == 0)
    def _():
        carry[...] = jnp.zeros(carry.shape, carry.dtype)

    lane = lax.broadcasted_iota(jnp.int32, (tr, LANES), 1)
    lane_f = lane.astype(F32)
    neg = jnp.float32(-jnp.inf)
    lg = jnp.where(lane < n_exp, lg_ref[...], neg)
    sels, tops, ids = [], [], []
    for _ in range(TOP_K):
        m = jnp.max(lg, axis=-1, keepdims=True)
        idx = jnp.min(jnp.where(lg == m, lane_f, float(LANES)), axis=-1, keepdims=True)
        sel = lane_f == idx
        lg = jnp.where(sel, neg, lg)
        sels.append(sel)
        tops.append(m)
        ids.append(idx)
    exps = [jnp.exp(m - tops[0]) for m in tops]
    denom = exps[0]
    for e in exps[1:]:
        denom = denom + e
    onehot = jnp.where(sels[0], 1.0, 0.0)
    for sel in sels[1:]:
        onehot = onehot + jnp.where(sel, 1.0, 0.0)
    rr = lax.broadcasted_iota(jnp.int32, (tr, tr), 0)
    cc = lax.broadcasted_iota(jnp.int32, (tr, tr), 1)
    earlier = jnp.where(cc < rr, 1.0, 0.0).astype(BF16)
    prefix = jnp.dot(earlier, onehot.astype(BF16), preferred_element_type=F32) + carry[0:1, :]
    out = jnp.zeros((tr, LANES), F32)
    for k in range(TOP_K):
        rank = jnp.sum(jnp.where(sels[k], prefix, 0.0), axis=-1, keepdims=True)
        out = jnp.where(lane == k, exps[k] / denom, out)
        out = jnp.where(lane == TOP_K + k, ids[k], out)
        out = jnp.where(lane == 2 * TOP_K + k, rank, out)
    o_ref[...] = out
    carry[0:1, :] = carry[0:1, :] + jnp.sum(onehot, axis=0, keepdims=True)
    cnt_ref[...] = carry[...]


def _router_call(logits, n_exp):
    n = logits.shape[0]
    tr = min(n, ROUTER_ROWS)
    return pl.pallas_call(
        functools.partial(_router_kernel, n_exp=n_exp),
        out_shape=[jax.ShapeDtypeStruct((n, LANES), F32),
                   jax.ShapeDtypeStruct((SUBLANES, LANES), F32)],
        grid=(n // tr,),
        in_specs=[pl.BlockSpec((tr, LANES), lambda i: (i, 0))],
        out_specs=[pl.BlockSpec((tr, LANES), lambda i: (i, 0)),
                   pl.BlockSpec((SUBLANES, LANES), lambda i: (0, 0))],
        scratch_shapes=[pltpu.VMEM((SUBLANES, LANES), F32)],
        compiler_params=pltpu.CompilerParams(dimension_semantics=("arbitrary",),
                                             vmem_limit_bytes=VMEM_LIMIT),
        name="router_topk",
    )(logits)


def _route(logits, n_exp, tm):
    n = logits.shape[0]
    nk = n * TOP_K
    routed, cnt = _router_call(logits, n_exp)
    top_w = routed[:, 0:TOP_K]
    top_idx = routed[:, TOP_K:2 * TOP_K].astype(jnp.int32)
    rank = routed[:, 2 * TOP_K:3 * TOP_K].astype(jnp.int32)
    counts = cnt[0, :n_exp].astype(jnp.int32)
    padded = (counts + tm - 1) // tm * tm
    pend = jnp.cumsum(padded)
    pstart = pend - padded
    dest = pstart[top_idx] + rank
    n_rows = (nk + tm - 1) // tm * tm + n_exp * tm
    n_blocks = n_rows // tm
    tok = jnp.broadcast_to(jnp.arange(n, dtype=jnp.int32)[:, None], (n, TOP_K))
    row_token = (jnp.arange(n_rows, dtype=jnp.int32) % n).at[dest.reshape(-1)].set(tok.reshape(-1))
    n_used = (pend[-1] // tm).astype(jnp.int32)
    blk = jnp.arange(n_blocks, dtype=jnp.int32)
    blk_start = jnp.minimum(blk, n_used - 1) * tm
    block_e = jnp.minimum(jnp.sum(pend[None, :] <= blk_start[:, None], axis=1), n_exp - 1).astype(jnp.int32)
    first = ((blk < n_used) & ((blk == 0) | (block_e != jnp.roll(block_e, 1)))).astype(jnp.int32)
    ids = jnp.where(counts > 0, jnp.arange(n_exp, dtype=jnp.int32), n_exp)
    after = jnp.concatenate([lax.cummin(ids, axis=0, reverse=True)[1:], jnp.full((1,), n_exp, jnp.int32)])
    nxt = jnp.where(after < n_exp, after, -1)[block_e].astype(jnp.int32)
    plan = (block_e, first, nxt, n_used.reshape(1))
    return top_w, dest.astype(jnp.int32), row_token, plan


def kernel(x, c, w_ada, b_ada, ada_table, w_in, conv_a, dw_b_w, dw_b_b, ln_b_g, ln_b_b, ln_c_g, ln_c_b, sgu_w, sgu_b, w_out_a, w_out_b, w_out_c, w_gate, b_gate, w_o, ln1_g, ln1_b, w_router, b_router, w_gu, b_gu, w_down, b_down, ln2_g, ln2_b):
    bsz, seq, d = x.shape
    depth = w_in.shape[0]
    n_exp = w_router.shape[-1]
    w = conv_a.shape[-1]
    alpha = (2.0 * depth) ** 0.25
    n = bsz * seq

    mod_shared = _ada_call(c, w_ada, b_ada).reshape(bsz, N_MOD, d)
    w_cat = jnp.concatenate([w_in, w_gate], axis=-1).astype(BF16)
    p = dict(conv_a=conv_a, dw_b_w=dw_b_w, dw_b_b=dw_b_b, ln_b_g=ln_b_g, ln_b_b=ln_b_b,
             ln_c_g=ln_c_g, ln_c_b=ln_c_b, sgu_w=sgu_w, sgu_b=sgu_b,
             w_out_a=w_out_a.astype(BF16), w_out_b=w_out_b.astype(BF16),
             w_out_c=w_out_c.astype(BF16), b_gate=b_gate, w_o=w_o.astype(BF16),
             ln1_g=ln1_g, ln1_b=ln1_b, w_router=w_router, b_router=b_router)
    tm = MOE_ROWS

    for l in range(depth):
        mod = mod_shared + ada_table[l][None]
        proj = _proj_call(x, mod, w_cat, l, w)
        x, h2, logits = _mixer_call(proj, x, mod, p, l, alpha)

        top_w, dest, row_token, plan = _route(logits.reshape(n, -1), n_exp, tm)
        xb = h2.reshape(n, d)[row_token]
        yb = _moe_call(plan, xb, w_gu, b_gu, w_down, b_down, l, tm)
        x = _ln2_call(x, yb, dest, top_w.reshape(bsz, seq, TOP_K), mod, ln2_g[l], ln2_b[l], alpha)
    return x
```

```python
import functools

import jax
import jax.numpy as jnp
from jax import lax
from jax.experimental import pallas as pl
from jax.experimental.pallas import tpu as pltpu

F32 = jnp.float32
BF16 = jnp.bfloat16
U32 = jnp.uint32

CHUNK = 64
SGU_BLOCK = 128
TOP_K = 4
SWIGLU_LIMIT = 7.0
SWIGLU_ALPHA = 1.702
LN_EPS = 1e-5
N_MOD = 6

LANES = 128
SUBLANES = 8
CONV_ROWS = 64
CONV_COLS = 2 * LANES
CONV_HALO = 32
ROUTER_LANES = LANES
VMEM_LIMIT = 56 * 1024 * 1024

SEQ_TILE = 256
PROJ_ROWS = 1024
MOE_ROWS = 512
ROUTER_ROWS = 512


def _layer_norm(x, g, b):
    mu = jnp.mean(x, axis=-1, keepdims=True)
    xc = x - mu
    var = jnp.mean(xc * xc, axis=-1, keepdims=True)
    return xc * lax.rsqrt(var + LN_EPS) * g + b


def _gelu(x):
    c = 0.7978845608028654
    return 0.5 * x * (1.0 + jnp.tanh(c * (x + 0.044715 * (x * x * x))))


def _sigmoid(x):
    return 1.0 / (1.0 + jnp.exp(-x))


def _pack_halves(y):
    half = y.shape[1] // 2
    hi = lax.bitcast_convert_type(y[:, :half].astype(BF16).astype(F32), U32)
    lo = lax.bitcast_convert_type(y[:, half:].astype(BF16).astype(F32), U32)
    return hi | (lo >> 16)


def _unpack_halves(p):
    hi = lax.bitcast_convert_type(p & jnp.uint32(0xFFFF0000), F32)
    lo = lax.bitcast_convert_type(p << 16, F32)
    return hi, lo


def _ada_kernel(c_ref, w_ref, b_ref, o_ref):
    c = c_ref[...]
    a = c * _sigmoid(c)
    o_ref[...] = jnp.dot(a, w_ref[...], preferred_element_type=F32,
                         precision=lax.Precision.HIGHEST) + b_ref[...]


def _ada_call(c, w_ada, b_ada):
    bsz, d = c.shape
    nc = w_ada.shape[1]
    bn = min(nc, 1024)
    return pl.pallas_call(
        _ada_kernel,
        out_shape=jax.ShapeDtypeStruct((bsz, nc), F32),
        grid=(nc // bn,),
        in_specs=[pl.BlockSpec((bsz, d), lambda j: (0, 0)),
                  pl.BlockSpec((d, bn), lambda j: (0, j)),
                  pl.BlockSpec((1, bn), lambda j: (0, j))],
        out_specs=pl.BlockSpec((bsz, bn), lambda j: (0, j)),
        compiler_params=pltpu.CompilerParams(dimension_semantics=("parallel",),
                                             vmem_limit_bytes=VMEM_LIMIT),
        name="ada_proj",
    )(c, w_ada, b_ada.reshape(1, nc))


def _proj_kernel(x_ref, mod_ref, w_ref, o_ref, h_ref):
    @pl.when(pl.program_id(2) == 0)
    def _():
        sh = mod_ref[0, 0:1, :]
        sc = mod_ref[0, 1:2, :]
        h_ref[...] = (x_ref[0] * (1.0 + sc) + sh).astype(BF16)

    o_ref[0] = jnp.dot(h_ref[...], w_ref[...], preferred_element_type=F32).astype(o_ref.dtype)


def _proj_call(x, mod, w_cat, layer, bn):
    bsz, seq, d = x.shape
    nc = w_cat.shape[-1]
    bm = min(seq, PROJ_ROWS)
    return pl.pallas_call(
        _proj_kernel,
        out_shape=jax.ShapeDtypeStruct((bsz, seq, nc), BF16),
        grid=(bsz, seq // bm, nc // bn),
        in_specs=[pl.BlockSpec((1, bm, d), lambda b, i, j: (b, i, 0)),
                  pl.BlockSpec((1, N_MOD, d), lambda b, i, j: (b, 0, 0)),
                  pl.BlockSpec((None, d, bn), lambda b, i, j: (layer, 0, j))],
        out_specs=pl.BlockSpec((1, bm, bn), lambda b, i, j: (b, i, j)),
        scratch_shapes=[pltpu.VMEM((bm, d), BF16)],
        compiler_params=pltpu.CompilerParams(
            dimension_semantics=("parallel", "parallel", "arbitrary"),
            vmem_limit_bytes=VMEM_LIMIT),
        name="mixer_in_proj",
    )(x, mod, w_cat)


def _mixer_kernel(proj_ref, halo_ref, x_ref, mod_ref, ca_ref, dww_ref, dwb_ref,
                  lnbg_ref, lnbb_ref, lncg_ref, lncb_ref, sw_ref, sb_ref,
                  woa_ref, wob_ref, woc_ref, bg_ref, wo_ref, ln1g_ref, ln1b_ref,
                  wr_ref, br_ref,
                  xo_ref, h2_ref, lg_ref,
                  abuf, bbuf, cbuf, sbuf, *, alpha):
    t = x_ref.shape[1]
    d = x_ref.shape[2]
    w = ca_ref.shape[1]
    n_heads = sw_ref.shape[0]
    hd = w // n_heads
    ka = ca_ref.shape[0]
    kb = dww_ref.shape[0]
    hal = CONV_HALO

    def pf(k):
        return proj_ref[0, :, k * w:(k + 1) * w].astype(F32)

    def hf(k):
        return halo_ref[0, :, k * w:(k + 1) * w].astype(F32)

    keep = jnp.where(pl.program_id(1) > 0, 1.0, 0.0).astype(F32)

    abuf[0:hal, :] = hf(0) * hf(2) * keep
    abuf[hal:hal + t, :] = pf(0) * pf(2)
    conv_a = ca_ref[0:1, :] * abuf[hal - ka + 1:hal - ka + 1 + t, :]
    for k in range(1, ka):
        conv_a = conv_a + ca_ref[k:k + 1, :] * abuf[hal - ka + 1 + k:hal - ka + 1 + k + t, :]
    y_a = (pf(1) * conv_a).astype(BF16)
    o_a = jnp.dot(y_a, woa_ref[...], preferred_element_type=F32)

    bbuf[0:hal, :] = hf(3) * _sigmoid(hf(4)) * keep
    bbuf[hal:hal + t, :] = pf(3) * _sigmoid(pf(4))
    rows = CONV_ROWS
    cols = sbuf.shape[2]
    span = sbuf.shape[1]
    for c0 in range(0, w, cols):
        for r in range(1, SUBLANES):
            sbuf[r - 1] = bbuf[r:r + span, c0:c0 + cols]
        for r0 in range(0, t, rows):
            acc = None
            for k in range(kb):
                q, r = divmod(hal - kb + 1 + k, SUBLANES)
                lo = r0 + SUBLANES * q
                if r == 0:
                    src = bbuf[lo:lo + rows, c0:c0 + cols]
                else:
                    src = sbuf[r - 1, lo:lo + rows, :]
                term = dww_ref[k:k + 1, c0:c0 + cols] * src
                acc = term if acc is None else acc + term
            cbuf[r0:r0 + rows, c0:c0 + cols] = acc
    y_b = _layer_norm(cbuf[...] + dwb_ref[...], lnbg_ref[...], lnbb_ref[...])
    y_b = (y_b * _sigmoid(y_b)).astype(BF16)
    o_b = jnp.dot(y_b, wob_ref[...], preferred_element_type=F32)

    v = _layer_norm(_gelu(pf(6)), lncg_ref[...], lncb_ref[...]).astype(BF16)
    ri = lax.broadcasted_iota(jnp.int32, (SGU_BLOCK, SGU_BLOCK), 0) // CHUNK
    ci = lax.broadcasted_iota(jnp.int32, (SGU_BLOCK, SGU_BLOCK), 1) // CHUNK
    causal = ri >= ci
    for g in range(n_heads):
        wm = jnp.where(causal, sw_ref[g], 0.0).astype(BF16)
        for blk in range(t // SGU_BLOCK):
            r0 = blk * SGU_BLOCK
            cbuf[r0:r0 + SGU_BLOCK, g * hd:(g + 1) * hd] = (
                jnp.dot(wm, v[r0:r0 + SGU_BLOCK, g * hd:(g + 1) * hd], preferred_element_type=F32)
                + sb_ref[:, g * hd:(g + 1) * hd])
    y_c = (_gelu(pf(5)) * cbuf[...]).astype(BF16)
    o_c = jnp.dot(y_c, woc_ref[...], preferred_element_type=F32)

    g0 = 7 * w

    def gate(k):
        z = proj_ref[0, :, g0 + k * d:g0 + (k + 1) * d].astype(F32) + bg_ref[:, k * d:(k + 1) * d]
        return _sigmoid(z)

    merged = (gate(0) * o_a + gate(1) * o_b + gate(2) * o_c).astype(BF16)
    mix = jnp.dot(merged, wo_ref[...], preferred_element_type=F32)
    xn = _layer_norm(alpha * x_ref[0] + mod_ref[0, 2:3, :] * mix, ln1g_ref[...], ln1b_ref[...])
    xo_ref[0] = xn

    h2 = xn * (1.0 + mod_ref[0, 4:5, :]) + mod_ref[0, 3:4, :]
    h2_hi = h2.astype(BF16)
    h2_ref[0] = h2_hi
    h2_lo = (h2 - h2_hi.astype(F32)).astype(BF16)
    p_hi = jnp.dot(h2_hi, wr_ref[...], preferred_element_type=F32)
    p_lo = jnp.dot(h2_lo, wr_ref[:, 0:ROUTER_LANES], preferred_element_type=F32)
    lg_ref[0] = p_hi[:, 0:ROUTER_LANES] + p_hi[:, ROUTER_LANES:] + p_lo + br_ref[...]


def _const_spec(shape, index):
    return pl.BlockSpec(shape, index, pipeline_mode=pl.Buffered(1))


def _mixer_call(proj, x, mod, p, layer, alpha):
    bsz, seq, d = x.shape
    w = p["conv_a"].shape[-1]
    nc = proj.shape[-1]
    t = min(seq, SEQ_TILE)
    hal = CONV_HALO
    hb = t // hal
    n_heads = p["sgu_w"].shape[1]
    row = lambda a: a[layer].reshape(1, -1)
    grid = (bsz, seq // t)
    cst = lambda *shape: _const_spec(shape, lambda b, s: (0,) * len(shape))
    lyr = lambda *shape: _const_spec((None,) + shape, lambda b, s: (layer,) + (0,) * len(shape))
    in_specs = [
        pl.BlockSpec((1, t, nc), lambda b, s: (b, s, 0)),
        pl.BlockSpec((1, hal, 7 * w), lambda b, s: (b, jnp.maximum(s * hb - 1, 0), 0)),
        pl.BlockSpec((1, t, d), lambda b, s: (b, s, 0)),
        pl.BlockSpec((1, N_MOD, d), lambda b, s: (b, 0, 0)),
        lyr(p["conv_a"].shape[1], w),
        lyr(p["dw_b_w"].shape[1], w),
        cst(1, w), cst(1, w), cst(1, w), cst(1, w), cst(1, w),
        lyr(n_heads, SGU_BLOCK, SGU_BLOCK),
        cst(SGU_BLOCK, w),
        lyr(w, d), lyr(w, d), lyr(w, d),
        cst(1, 3 * d),
        lyr(d, d),
        cst(1, d), cst(1, d),
        cst(d, 2 * ROUTER_LANES), cst(1, ROUTER_LANES),
    ]
    out_specs = [
        pl.BlockSpec((1, t, d), lambda b, s: (b, s, 0)),
        pl.BlockSpec((1, t, d), lambda b, s: (b, s, 0)),
        pl.BlockSpec((1, t, ROUTER_LANES), lambda b, s: (b, s, 0)),
    ]
    out_shape = [
        jax.ShapeDtypeStruct((bsz, seq, d), F32),
        jax.ShapeDtypeStruct((bsz, seq, d), BF16),
        jax.ShapeDtypeStruct((bsz, seq, ROUTER_LANES), F32),
    ]
    n_exp = p["w_router"].shape[-1]
    wr = jnp.pad(p["w_router"][layer], ((0, 0), (0, ROUTER_LANES - n_exp)))
    wr_hi = wr.astype(BF16)
    wr = jnp.concatenate([wr_hi, (wr - wr_hi.astype(F32)).astype(BF16)], axis=1)
    br = jnp.pad(p["b_router"][layer], (0, ROUTER_LANES - n_exp)).reshape(1, -1)
    sb = jnp.repeat(p["sgu_b"][layer].T, w // n_heads, axis=1)
    return pl.pallas_call(
        functools.partial(_mixer_kernel, alpha=alpha),
        out_shape=out_shape,
        grid=grid,
        in_specs=in_specs,
        out_specs=out_specs,
        scratch_shapes=[pltpu.VMEM((t + hal, w), F32), pltpu.VMEM((t + hal, w), F32),
                        pltpu.VMEM((t, w), F32),
                        pltpu.VMEM((SUBLANES - 1, t + hal - SUBLANES, min(w, CONV_COLS)), F32)],
        compiler_params=pltpu.CompilerParams(dimension_semantics=("parallel", "parallel"),
                                             vmem_limit_bytes=VMEM_LIMIT),
        name="mixer",
    )(proj, proj, x, mod, p["conv_a"], p["dw_b_w"], row(p["dw_b_b"]),
      row(p["ln_b_g"]), row(p["ln_b_b"]), row(p["ln_c_g"]), row(p["ln_c_b"]),
      p["sgu_w"], sb, p["w_out_a"], p["w_out_b"], p["w_out_c"], row(p["b_gate"]),
      p["w_o"], row(p["ln1_g"]), row(p["ln1_b"]), wr, br)


def _moe_kernel(be_ref, first_ref, nxt_ref, nb_ref, x_ref, wgu_hbm, bgu_ref, wd_hbm, bd_ref, o_ref,
                gu_stage, d_stage, wgu_bf, wd_bf, sem, *, layer):
    f = wd_bf.shape[0]
    i = pl.program_id(0)

    def weight_copies(expert):
        return (pltpu.make_async_copy(wgu_hbm.at[layer, expert], gu_stage, sem.at[0]),
                pltpu.make_async_copy(wd_hbm.at[layer, expert], d_stage, sem.at[1]))

    @pl.when(i == 0)
    def _():
        for cp in weight_copies(be_ref[0]):
            cp.start()

    @pl.when(first_ref[i] == 1)
    def _():
        for cp in weight_copies(be_ref[i]):
            cp.wait()
        wgu_bf[...] = gu_stage[...].astype(BF16)
        wd_bf[...] = d_stage[...].astype(BF16)

        @pl.when(nxt_ref[i] >= 0)
        def _():
            for cp in weight_copies(nxt_ref[i]):
                cp.start()

    @pl.when(i < nb_ref[0])
    def _():
        gu = jnp.dot(x_ref[...], wgu_bf[...], preferred_element_type=F32) + bgu_ref[...]
        gate = jnp.minimum(gu[:, :f], SWIGLU_LIMIT)
        up = jnp.clip(gu[:, f:], -SWIGLU_LIMIT, SWIGLU_LIMIT)
        act = (up + 1.0) * (gate * _sigmoid(SWIGLU_ALPHA * gate))
        y = jnp.dot(act.astype(BF16), wd_bf[...], preferred_element_type=F32) + bd_ref[...]
        o_ref[...] = _pack_halves(y)

    @pl.when(i >= nb_ref[0])
    def _():
        o_ref[...] = jnp.zeros(o_ref.shape, o_ref.dtype)


def _moe_call(plan, xb, w_gu, b_gu, w_down, b_down, layer, tm):
    n_rows, d = xb.shape
    n_exp, _, f2 = w_gu.shape[1:]
    f = f2 // 2
    grid_spec = pltpu.PrefetchScalarGridSpec(
        num_scalar_prefetch=4,
        grid=(n_rows // tm,),
        in_specs=[
            pl.BlockSpec((tm, d), lambda i, *_: (i, 0)),
            pl.BlockSpec(memory_space=pl.ANY),
            pl.BlockSpec((None, None, 1, f2), lambda i, be, *_: (layer, be[i], 0, 0)),
            pl.BlockSpec(memory_space=pl.ANY),
            pl.BlockSpec((None, None, 1, d), lambda i, be, *_: (layer, be[i], 0, 0)),
        ],
        out_specs=pl.BlockSpec((tm, d // 2), lambda i, *_: (i, 0)),
        scratch_shapes=[pltpu.VMEM((d, f2), F32), pltpu.VMEM((f, d), F32),
                        pltpu.VMEM((d, f2), BF16), pltpu.VMEM((f, d), BF16),
                        pltpu.SemaphoreType.DMA((2,))],
    )
    return pl.pallas_call(
        functools.partial(_moe_kernel, layer=layer),
        out_shape=jax.ShapeDtypeStruct((n_rows, d // 2), U32),
        grid_spec=grid_spec,
        compiler_params=pltpu.CompilerParams(dimension_semantics=("arbitrary",),
                                             vmem_limit_bytes=VMEM_LIMIT),
        name="moe_experts",
    )(*plan, xb, w_gu, b_gu.reshape(b_gu.shape[0], n_exp, 1, f2),
      w_down, b_down.reshape(b_down.shape[0], n_exp, 1, d))


def _ln2_kernel(dcur_ref, dnxt_ref, x_ref, yb_ref, tw_ref, mod_ref, g_ref, b_ref, o_ref,
                buf_even, buf_odd, sem, *, alpha):
    t = x_ref.shape[1]
    n_rows_step = TOP_K * t
    i = pl.program_id(0)

    def start_gather(d_ref, dst, s):
        for r in range(n_rows_step):
            pltpu.make_async_copy(yb_ref.at[pl.ds(d_ref[0, 0, r], 1)], dst.at[pl.ds(r, 1)],
                                  sem.at[s]).start()

    def wait_gather(dst, s):
        pltpu.make_async_copy(yb_ref.at[pl.ds(0, n_rows_step)], dst, sem.at[s]).wait()

    def combine(src):
        tw = tw_ref[0]
        y_hi = y_lo = None
        for k in range(TOP_K):
            hi, lo = _unpack_halves(src[k * t:(k + 1) * t, :])
            wk = tw[:, k:k + 1]
            y_hi = wk * hi if y_hi is None else y_hi + wk * hi
            y_lo = wk * lo if y_lo is None else y_lo + wk * lo
        y = jnp.concatenate([y_hi, y_lo], axis=1)
        o_ref[0] = _layer_norm(alpha * x_ref[0] + mod_ref[0, 5:6, :] * y, g_ref[...], b_ref[...])

    @pl.when(i == 0)
    def _():
        start_gather(dcur_ref, buf_even, 0)

    def step(cur, cur_sem, nxt, nxt_sem):
        wait_gather(cur, cur_sem)
        start_gather(dnxt_ref, nxt, nxt_sem)
        combine(cur)

        @pl.when(i == pl.num_programs(0) - 1)
        def _():
            wait_gather(nxt, nxt_sem)

    @pl.when(lax.rem(i, 2) == 0)
    def _():
        step(buf_even, 0, buf_odd, 1)

    @pl.when(lax.rem(i, 2) == 1)
    def _():
        step(buf_odd, 1, buf_even, 0)


def _ln2_call(x, yb, dest, top_w, mod, g, b, alpha):
    bsz, seq, d = x.shape
    t = min(seq, SEQ_TILE)
    spb = seq // t
    nst = bsz * spb
    dsteps = dest.reshape(nst, t, TOP_K).transpose(0, 2, 1).reshape(nst, 1, TOP_K * t)
    smem = lambda im: pl.BlockSpec((1, 1, TOP_K * t), im, memory_space=pltpu.SMEM)
    tile = lambda last: pl.BlockSpec((1, t, last), lambda i: (i // spb, i % spb, 0))
    return pl.pallas_call(
        functools.partial(_ln2_kernel, alpha=alpha),
        out_shape=jax.ShapeDtypeStruct((bsz, seq, d), F32),
        grid=(nst,),
        in_specs=[smem(lambda i: (i, 0, 0)),
                  smem(lambda i: (jnp.minimum(i + 1, nst - 1), 0, 0)),
                  tile(d),
                  pl.BlockSpec(memory_space=pl.ANY),
                  tile(TOP_K),
                  pl.BlockSpec((1, N_MOD, d), lambda i: (i // spb, 0, 0)),
                  pl.BlockSpec((1, d), lambda i: (0, 0)),
                  pl.BlockSpec((1, d), lambda i: (0, 0))],
        out_specs=tile(d),
        scratch_shapes=[pltpu.VMEM((TOP_K * t, d // 2), U32), pltpu.VMEM((TOP_K * t, d // 2), U32),
                        pltpu.SemaphoreType.DMA((2,))],
        compiler_params=pltpu.CompilerParams(dimension_semantics=("arbitrary",),
                                             vmem_limit_bytes=VMEM_LIMIT),
        name="moe_combine_ln",
    )(dsteps, dsteps, x, yb, top_w, mod, g.reshape(1, d), b.reshape(1, d))


def _router_kernel(lg_ref, o_ref, cnt_ref, carry, *, n_exp):
    tr = lg_ref.shape[0]

    @pl.when(pl.program_id(0) == 0)
    def _():
        carry[...] = jnp.zeros(carry.shape, carry.dtype)

    lane = lax.broadcasted_iota(jnp.int32, (tr, LANES), 1)
    lane_f = lane.astype(F32)
    neg = jnp.float32(-jnp.inf)
    lg = jnp.where(lane < n_exp, lg_ref[...], neg)
    sels, tops, ids = [], [], []
    for _ in range(TOP_K):
        m = jnp.max(lg, axis=-1, keepdims=True)
        idx = jnp.min(jnp.where(lg == m, lane_f, float(LANES)), axis=-1, keepdims=True)
        sel = lane_f == idx
        lg = jnp.where(sel, neg, lg)
        sels.append(sel)
        tops.append(m)
        ids.append(idx)
    exps = [jnp.exp(m - tops[0]) for m in tops]
    denom = exps[0]
    for e in exps[1:]:
        denom = denom + e
    onehot = jnp.where(sels[0], 1.0, 0.0)
    for sel in sels[1:]:
        onehot = onehot + jnp.where(sel, 1.0, 0.0)
    rr = lax.broadcasted_iota(jnp.int32, (tr, tr), 0)
    cc = lax.broadcasted_iota(jnp.int32, (tr, tr), 1)
    earlier = jnp.where(cc < rr, 1.0, 0.0).astype(BF16)
    prefix = jnp.dot(earlier, onehot.astype(BF16), preferred_element_type=F32) + carry[0:1, :]
    out = jnp.zeros((tr, LANES), F32)
    for k in range(TOP_K):
        rank = jnp.sum(jnp.where(sels[k], prefix, 0.0), axis=-1, keepdims=True)
        out = jnp.where(lane == k, exps[k] / denom, out)
        out = jnp.where(lane == TOP_K + k, ids[k], out)
        out = jnp.where(lane == 2 * TOP_K + k, rank, out)
    o_ref[...] = out
    carry[0:1, :] = carry[0:1, :] + jnp.sum(onehot, axis=0, keepdims=True)
    cnt_ref[...] = carry[...]


def _router_call(logits, n_exp):
    n = logits.shape[0]
    tr = min(n, ROUTER_ROWS)
    return pl.pallas_call(
        functools.partial(_router_kernel, n_exp=n_exp),
        out_shape=[jax.ShapeDtypeStruct((n, LANES), F32),
                   jax.ShapeDtypeStruct((SUBLANES, LANES), F32)],
        grid=(n // tr,),
        in_specs=[pl.BlockSpec((tr, LANES), lambda i: (i, 0))],
        out_specs=[pl.BlockSpec((tr, LANES), lambda i: (i, 0)),
                   pl.BlockSpec((SUBLANES, LANES), lambda i: (0, 0))],
        scratch_shapes=[pltpu.VMEM((SUBLANES, LANES), F32)],
        compiler_params=pltpu.CompilerParams(dimension_semantics=("arbitrary",),
                                             vmem_limit_bytes=VMEM_LIMIT),
        name="router_topk",
    )(logits)


def _route(logits, n_exp, tm):
    n = logits.shape[0]
    nk = n * TOP_K
    routed, cnt = _router_call(logits, n_exp)
    top_w = routed[:, 0:TOP_K]
    top_idx = routed[:, TOP_K:2 * TOP_K].astype(jnp.int32)
    rank = routed[:, 2 * TOP_K:3 * TOP_K].astype(jnp.int32)
    counts = cnt[0, :n_exp].astype(jnp.int32)
    padded = (counts + tm - 1) // tm * tm
    pend = jnp.cumsum(padded)
    pstart = pend - padded
    dest = pstart[top_idx] + rank
    n_rows = (nk + tm - 1) // tm * tm + n_exp * tm
    n_blocks = n_rows // tm
    n_used = (pend[-1] // tm).astype(jnp.int32)
    blk = jnp.arange(n_blocks, dtype=jnp.int32)
    blk_start = jnp.minimum(blk, n_used - 1) * tm
    block_e = jnp.minimum(jnp.sum(pend[None, :] <= blk_start[:, None], axis=1), n_exp - 1).astype(jnp.int32)
    tok = jnp.broadcast_to(jnp.arange(n, dtype=jnp.int32)[:, None], (n, TOP_K))
    _, tok_by_row = lax.sort((dest.reshape(-1), tok.reshape(-1)), num_keys=1)
    start = jnp.cumsum(counts) - counts
    rows = jnp.arange(n_rows, dtype=jnp.int32).reshape(n_blocks, tm)
    in_expert = rows - pstart[block_e][:, None]
    occupied = (in_expert < counts[block_e][:, None]) & (blk < n_used)[:, None]
    src = jnp.clip(start[block_e][:, None] + in_expert, 0, nk - 1)
    row_token = jnp.where(occupied, tok_by_row[src], rows % n).reshape(n_rows)
    first = ((blk < n_used) & ((blk == 0) | (block_e != jnp.roll(block_e, 1)))).astype(jnp.int32)
    ids = jnp.where(counts > 0, jnp.arange(n_exp, dtype=jnp.int32), n_exp)
    after = jnp.concatenate([lax.cummin(ids, axis=0, reverse=True)[1:], jnp.full((1,), n_exp, jnp.int32)])
    nxt = jnp.where(after < n_exp, after, -1)[block_e].astype(jnp.int32)
    plan = (block_e, first, nxt, n_used.reshape(1))
    return top_w, dest.astype(jnp.int32), row_token, plan


def kernel(x, c, w_ada, b_ada, ada_table, w_in, conv_a, dw_b_w, dw_b_b, ln_b_g, ln_b_b, ln_c_g, ln_c_b, sgu_w, sgu_b, w_out_a, w_out_b, w_out_c, w_gate, b_gate, w_o, ln1_g, ln1_b, w_router, b_router, w_gu, b_gu, w_down, b_down, ln2_g, ln2_b):
    bsz, seq, d = x.shape
    depth = w_in.shape[0]
    n_exp = w_router.shape[-1]
    w = conv_a.shape[-1]
    alpha = (2.0 * depth) ** 0.25
    n = bsz * seq

    mod_shared = _ada_call(c, w_ada, b_ada).reshape(bsz, N_MOD, d)
    w_cat = jnp.concatenate([w_in, w_gate], axis=-1).astype(BF16)
    p = dict(conv_a=conv_a, dw_b_w=dw_b_w, dw_b_b=dw_b_b, ln_b_g=ln_b_g, ln_b_b=ln_b_b,
             ln_c_g=ln_c_g, ln_c_b=ln_c_b, sgu_w=sgu_w, sgu_b=sgu_b,
             w_out_a=w_out_a.astype(BF16), w_out_b=w_out_b.astype(BF16),
             w_out_c=w_out_c.astype(BF16), b_gate=b_gate, w_o=w_o.astype(BF16),
             ln1_g=ln1_g, ln1_b=ln1_b, w_router=w_router, b_router=b_router)
    tm = MOE_ROWS

    for l in range(depth):
        mod = mod_shared + ada_table[l][None]
        proj = _proj_call(x, mod, w_cat, l, w)
        x, h2, logits = _mixer_call(proj, x, mod, p, l, alpha)

        top_w, dest, row_token, plan = _route(logits.reshape(n, -1), n_exp, tm)
        xb = h2.reshape(n, d)[row_token]
        yb = _moe_call(plan, xb, w_gu, b_gu, w_down, b_down, l, tm)
        x = _ln2_call(x, yb, dest, top_w.reshape(bsz, seq, TOP_K), mod, ln2_g[l], ln2_b[l], alpha)
    return x
```

```python
import functools

import jax
import jax.numpy as jnp
from jax import lax
from jax.experimental import pallas as pl
from jax.experimental.pallas import tpu as pltpu

F32 = jnp.float32
BF16 = jnp.bfloat16
U32 = jnp.uint32

CHUNK = 64
SGU_BLOCK = 128
TOP_K = 4
SWIGLU_LIMIT = 7.0
SWIGLU_ALPHA = 1.702
LN_EPS = 1e-5
N_MOD = 6

LANES = 128
SUBLANES = 8
CONV_ROWS = 64
CONV_COLS = 2 * LANES
CONV_HALO = 32
ROUTER_LANES = LANES
VMEM_LIMIT = 56 * 1024 * 1024

SEQ_TILE = 256
PROJ_ROWS = 1024
MOE_ROWS = 512
ROUTER_ROWS = 512


def _layer_norm(x, g, b):
    mu = jnp.mean(x, axis=-1, keepdims=True)
    xc = x - mu
    var = jnp.mean(xc * xc, axis=-1, keepdims=True)
    return xc * lax.rsqrt(var + LN_EPS) * g + b


def _gelu(x):
    c = 0.7978845608028654
    return 0.5 * x * (1.0 + jnp.tanh(c * (x + 0.044715 * (x * x * x))))


def _sigmoid(x):
    return 1.0 / (1.0 + jnp.exp(-x))


def _pack_halves(y):
    half = y.shape[1] // 2
    hi = lax.bitcast_convert_type(y[:, :half].astype(BF16).astype(F32), U32)
    lo = lax.bitcast_convert_type(y[:, half:].astype(BF16).astype(F32), U32)
    return hi | (lo >> 16)


def _unpack_halves(p):
    hi = lax.bitcast_convert_type(p & jnp.uint32(0xFFFF0000), F32)
    lo = lax.bitcast_convert_type(p << 16, F32)
    return hi, lo


def _ada_kernel(c_ref, w_ref, b_ref, o_ref):
    c = c_ref[...]
    a = c * _sigmoid(c)
    o_ref[...] = jnp.dot(a, w_ref[...], preferred_element_type=F32,
                         precision=lax.Precision.HIGHEST) + b_ref[...]


def _ada_call(c, w_ada, b_ada):
    bsz, d = c.shape
    nc = w_ada.shape[1]
    bn = min(nc, 1024)
    return pl.pallas_call(
        _ada_kernel,
        out_shape=jax.ShapeDtypeStruct((bsz, nc), F32),
        grid=(nc // bn,),
        in_specs=[pl.BlockSpec((bsz, d), lambda j: (0, 0)),
                  pl.BlockSpec((d, bn), lambda j: (0, j)),
                  pl.BlockSpec((1, bn), lambda j: (0, j))],
        out_specs=pl.BlockSpec((bsz, bn), lambda j: (0, j)),
        compiler_params=pltpu.CompilerParams(dimension_semantics=("parallel",),
                                             vmem_limit_bytes=VMEM_LIMIT),
        name="ada_proj",
    )(c, w_ada, b_ada.reshape(1, nc))


def _proj_kernel(x_ref, mod_ref, w_ref, o_ref, h_ref):
    @pl.when(pl.program_id(2) == 0)
    def _():
        sh = mod_ref[0, 0:1, :]
        sc = mod_ref[0, 1:2, :]
        h_ref[...] = (x_ref[0] * (1.0 + sc) + sh).astype(BF16)

    o_ref[0] = jnp.dot(h_ref[...], w_ref[...], preferred_element_type=F32).astype(o_ref.dtype)


def _proj_call(x, mod, w_cat, layer, bn):
    bsz, seq, d = x.shape
    nc = w_cat.shape[-1]
    bm = min(seq, PROJ_ROWS)
    return pl.pallas_call(
        _proj_kernel,
        out_shape=jax.ShapeDtypeStruct((bsz, seq, nc), BF16),
        grid=(bsz, seq // bm, nc // bn),
        in_specs=[pl.BlockSpec((1, bm, d), lambda b, i, j: (b, i, 0)),
                  pl.BlockSpec((1, N_MOD, d), lambda b, i, j: (b, 0, 0)),
                  pl.BlockSpec((None, d, bn), lambda b, i, j: (layer, 0, j))],
        out_specs=pl.BlockSpec((1, bm, bn), lambda b, i, j: (b, i, j)),
        scratch_shapes=[pltpu.VMEM((bm, d), BF16)],
        compiler_params=pltpu.CompilerParams(
            dimension_semantics=("parallel", "parallel", "arbitrary"),
            vmem_limit_bytes=VMEM_LIMIT),
        name="mixer_in_proj",
    )(x, mod, w_cat)


def _mixer_kernel(proj_ref, halo_ref, x_ref, mod_ref, ca_ref, dww_ref, dwb_ref,
                  lnbg_ref, lnbb_ref, lncg_ref, lncb_ref, sw_ref, sb_ref,
                  woa_ref, wob_ref, woc_ref, bg_ref, wo_ref, ln1g_ref, ln1b_ref,
                  wr_ref, br_ref,
                  xo_ref, h2_ref, lg_ref,
                  abuf, bbuf, cbuf, sbuf, *, alpha):
    t = x_ref.shape[1]
    d = x_ref.shape[2]
    w = ca_ref.shape[1]
    n_heads = sw_ref.shape[0]
    hd = w // n_heads
    ka = ca_ref.shape[0]
    kb = dww_ref.shape[0]
    hal = CONV_HALO

    def pf(k):
        return proj_ref[0, :, k * w:(k + 1) * w].astype(F32)

    def hf(k):
        return halo_ref[0, :, k * w:(k + 1) * w].astype(F32)

    keep = jnp.where(pl.program_id(1) > 0, 1.0, 0.0).astype(F32)

    abuf[0:hal, :] = hf(0) * hf(2) * keep
    abuf[hal:hal + t, :] = pf(0) * pf(2)
    conv_a = ca_ref[0:1, :] * abuf[hal - ka + 1:hal - ka + 1 + t, :]
    for k in range(1, ka):
        conv_a = conv_a + ca_ref[k:k + 1, :] * abuf[hal - ka + 1 + k:hal - ka + 1 + k + t, :]
    y_a = (pf(1) * conv_a).astype(BF16)
    o_a = jnp.dot(y_a, woa_ref[...], preferred_element_type=F32)

    bbuf[0:hal, :] = hf(3) * _sigmoid(hf(4)) * keep
    bbuf[hal:hal + t, :] = pf(3) * _sigmoid(pf(4))
    rows = CONV_ROWS
    cols = sbuf.shape[2]
    span = sbuf.shape[1]
    for c0 in range(0, w, cols):
        for r in range(1, SUBLANES):
            sbuf[r - 1] = bbuf[r:r + span, c0:c0 + cols]
        for r0 in range(0, t, rows):
            acc = None
            for k in range(kb):
                q, r = divmod(hal - kb + 1 + k, SUBLANES)
                lo = r0 + SUBLANES * q
                if r == 0:
                    src = bbuf[lo:lo + rows, c0:c0 + cols]
                else:
                    src = sbuf[r - 1, lo:lo + rows, :]
                term = dww_ref[k:k + 1, c0:c0 + cols] * src
                acc = term if acc is None else acc + term
            cbuf[r0:r0 + rows, c0:c0 + cols] = acc
    y_b = _layer_norm(cbuf[...] + dwb_ref[...], lnbg_ref[...], lnbb_ref[...])
    y_b = (y_b * _sigmoid(y_b)).astype(BF16)
    o_b = jnp.dot(y_b, wob_ref[...], preferred_element_type=F32)

    v = _layer_norm(_gelu(pf(6)), lncg_ref[...], lncb_ref[...]).astype(BF16)
    ri = lax.broadcasted_iota(jnp.int32, (SGU_BLOCK, SGU_BLOCK), 0) // CHUNK
    ci = lax.broadcasted_iota(jnp.int32, (SGU_BLOCK, SGU_BLOCK), 1) // CHUNK
    causal = ri >= ci
    for g in range(n_heads):
        wm = jnp.where(causal, sw_ref[g], 0.0).astype(BF16)
        for blk in range(t // SGU_BLOCK):
            r0 = blk * SGU_BLOCK
            cbuf[r0:r0 + SGU_BLOCK, g * hd:(g + 1) * hd] = (
                jnp.dot(wm, v[r0:r0 + SGU_BLOCK, g * hd:(g + 1) * hd], preferred_element_type=F32)
                + sb_ref[:, g * hd:(g + 1) * hd])
    y_c = (_gelu(pf(5)) * cbuf[...]).astype(BF16)
    o_c = jnp.dot(y_c, woc_ref[...], preferred_element_type=F32)

    g0 = 7 * w

    def gate(k):
        z = proj_ref[0, :, g0 + k * d:g0 + (k + 1) * d].astype(F32) + bg_ref[:, k * d:(k + 1) * d]
        return _sigmoid(z)

    merged = (gate(0) * o_a + gate(1) * o_b + gate(2) * o_c).astype(BF16)
    mix = jnp.dot(merged, wo_ref[...], preferred_element_type=F32)
    xn = _layer_norm(alpha * x_ref[0] + mod_ref[0, 2:3, :] * mix, ln1g_ref[...], ln1b_ref[...])
    xo_ref[0] = xn

    h2 = xn * (1.0 + mod_ref[0, 4:5, :]) + mod_ref[0, 3:4, :]
    h2_hi = h2.astype(BF16)
    h2_ref[0] = _pack_halves(h2)
    h2_lo = (h2 - h2_hi.astype(F32)).astype(BF16)
    p_hi = jnp.dot(h2_hi, wr_ref[...], preferred_element_type=F32)
    p_lo = jnp.dot(h2_lo, wr_ref[:, 0:ROUTER_LANES], preferred_element_type=F32)
    lg_ref[0] = p_hi[:, 0:ROUTER_LANES] + p_hi[:, ROUTER_LANES:] + p_lo + br_ref[...]


def _const_spec(shape, index):
    return pl.BlockSpec(shape, index, pipeline_mode=pl.Buffered(1))


def _mixer_call(proj, x, mod, p, layer, alpha):
    bsz, seq, d = x.shape
    w = p["conv_a"].shape[-1]
    nc = proj.shape[-1]
    t = min(seq, SEQ_TILE)
    hal = CONV_HALO
    hb = t // hal
    n_heads = p["sgu_w"].shape[1]
    row = lambda a: a[layer].reshape(1, -1)
    grid = (bsz, seq // t)
    cst = lambda *shape: _const_spec(shape, lambda b, s: (0,) * len(shape))
    lyr = lambda *shape: _const_spec((None,) + shape, lambda b, s: (layer,) + (0,) * len(shape))
    in_specs = [
        pl.BlockSpec((1, t, nc), lambda b, s: (b, s, 0)),
        pl.BlockSpec((1, hal, 7 * w), lambda b, s: (b, jnp.maximum(s * hb - 1, 0), 0)),
        pl.BlockSpec((1, t, d), lambda b, s: (b, s, 0)),
        pl.BlockSpec((1, N_MOD, d), lambda b, s: (b, 0, 0)),
        lyr(p["conv_a"].shape[1], w),
        lyr(p["dw_b_w"].shape[1], w),
        cst(1, w), cst(1, w), cst(1, w), cst(1, w), cst(1, w),
        lyr(n_heads, SGU_BLOCK, SGU_BLOCK),
        cst(SGU_BLOCK, w),
        lyr(w, d), lyr(w, d), lyr(w, d),
        cst(1, 3 * d),
        lyr(d, d),
        cst(1, d), cst(1, d),
        cst(d, 2 * ROUTER_LANES), cst(1, ROUTER_LANES),
    ]
    out_specs = [
        pl.BlockSpec((1, t, d), lambda b, s: (b, s, 0)),
        pl.BlockSpec((1, t, d // 2), lambda b, s: (b, s, 0)),
        pl.BlockSpec((1, t, ROUTER_LANES), lambda b, s: (b, s, 0)),
    ]
    out_shape = [
        jax.ShapeDtypeStruct((bsz, seq, d), F32),
        jax.ShapeDtypeStruct((bsz, seq, d // 2), U32),
        jax.ShapeDtypeStruct((bsz, seq, ROUTER_LANES), F32),
    ]
    n_exp = p["w_router"].shape[-1]
    wr = jnp.pad(p["w_router"][layer], ((0, 0), (0, ROUTER_LANES - n_exp)))
    wr_hi = wr.astype(BF16)
    wr = jnp.concatenate([wr_hi, (wr - wr_hi.astype(F32)).astype(BF16)], axis=1)
    br = jnp.pad(p["b_router"][layer], (0, ROUTER_LANES - n_exp)).reshape(1, -1)
    sb = jnp.repeat(p["sgu_b"][layer].T, w // n_heads, axis=1)
    return pl.pallas_call(
        functools.partial(_mixer_kernel, alpha=alpha),
        out_shape=out_shape,
        grid=grid,
        in_specs=in_specs,
        out_specs=out_specs,
        scratch_shapes=[pltpu.VMEM((t + hal, w), F32), pltpu.VMEM((t + hal, w), F32),
                        pltpu.VMEM((t, w), F32),
                        pltpu.VMEM((SUBLANES - 1, t + hal - SUBLANES, min(w, CONV_COLS)), F32)],
        compiler_params=pltpu.CompilerParams(dimension_semantics=("parallel", "parallel"),
                                             vmem_limit_bytes=VMEM_LIMIT),
        name="mixer",
    )(proj, proj, x, mod, p["conv_a"], p["dw_b_w"], row(p["dw_b_b"]),
      row(p["ln_b_g"]), row(p["ln_b_b"]), row(p["ln_c_g"]), row(p["ln_c_b"]),
      p["sgu_w"], sb, p["w_out_a"], p["w_out_b"], p["w_out_c"], row(p["b_gate"]),
      p["w_o"], row(p["ln1_g"]), row(p["ln1_b"]), wr, br)


def _moe_kernel(be_ref, first_ref, nxt_ref, nb_ref, rcur_ref, rnxt_ref, h_hbm, wgu_hbm, bgu_ref,
                wd_hbm, bd_ref, o_ref,
                gu_stage, d_stage, wgu_bf, wd_bf, x_even, x_odd, wsem, xsem, *, layer):
    f = wd_bf.shape[0]
    tm, half = x_even.shape
    i = pl.program_id(0)

    def weight_copies(expert):
        return (pltpu.make_async_copy(wgu_hbm.at[layer, expert], gu_stage, wsem.at[0]),
                pltpu.make_async_copy(wd_hbm.at[layer, expert], d_stage, wsem.at[1]))

    def row_copy(r_ref, r, dst, s):
        return pltpu.make_async_copy(h_hbm.at[pl.ds(r_ref[0, 0, r], 1)], dst.at[pl.ds(r, 1)],
                                     xsem.at[s])

    def start_rows_inline(r_ref, dst, s):
        for r in range(tm):
            row_copy(r_ref, r, dst, s).start()

    def start_rows_loop(r_ref, dst, s):
        def body(r, carry):
            row_copy(r_ref, r, dst, s).start()
            return carry
        lax.fori_loop(0, tm, body, 0)

    def wait_rows(dst, s):
        pltpu.make_async_copy(h_hbm.at[pl.ds(0, tm)], dst, xsem.at[s]).wait()

    @pl.when(i == 0)
    def _():
        for cp in weight_copies(be_ref[0]):
            cp.start()
        start_rows_loop(rcur_ref, x_even, 0)

    @pl.when(first_ref[i] == 1)
    def _():
        for cp in weight_copies(be_ref[i]):
            cp.wait()
        wgu_bf[...] = gu_stage[...].astype(BF16)
        wd_bf[...] = d_stage[...].astype(BF16)

        @pl.when(nxt_ref[i] >= 0)
        def _():
            for cp in weight_copies(nxt_ref[i]):
                cp.start()

    def step(cur, cur_sem, nxt, nxt_sem):
        wait_rows(cur, cur_sem)

        @pl.when(i < nb_ref[0])
        def _():
            start_rows_inline(rnxt_ref, nxt, nxt_sem)
            hi, lo = _unpack_halves(cur[...])
            gu = (jnp.dot(hi.astype(BF16), wgu_bf[0:half, :], preferred_element_type=F32)
                  + jnp.dot(lo.astype(BF16), wgu_bf[half:, :], preferred_element_type=F32)
                  + bgu_ref[...])
            gate = jnp.minimum(gu[:, :f], SWIGLU_LIMIT)
            up = jnp.clip(gu[:, f:], -SWIGLU_LIMIT, SWIGLU_LIMIT)
            act = (up + 1.0) * (gate * _sigmoid(SWIGLU_ALPHA * gate))
            y = jnp.dot(act.astype(BF16), wd_bf[...], preferred_element_type=F32) + bd_ref[...]
            o_ref[...] = _pack_halves(y)

        @pl.when(i >= nb_ref[0])
        def _():
            start_rows_loop(rnxt_ref, nxt, nxt_sem)
            o_ref[...] = jnp.zeros(o_ref.shape, o_ref.dtype)

        @pl.when(i == pl.num_programs(0) - 1)
        def _():
            wait_rows(nxt, nxt_sem)

    @pl.when(lax.rem(i, 2) == 0)
    def _():
        step(x_even, 0, x_odd, 1)

    @pl.when(lax.rem(i, 2) == 1)
    def _():
        step(x_odd, 1, x_even, 0)


def _moe_call(plan, row_token, h2p, w_gu, b_gu, w_down, b_down, layer, tm):
    half = h2p.shape[1]
    d = 2 * half
    n_rows = row_token.shape[0]
    nb = n_rows // tm
    n_exp, _, f2 = w_gu.shape[1:]
    f = f2 // 2
    rsteps = row_token.reshape(nb, 1, tm)
    smem = lambda im: pl.BlockSpec((1, 1, tm), im, memory_space=pltpu.SMEM)
    grid_spec = pltpu.PrefetchScalarGridSpec(
        num_scalar_prefetch=4,
        grid=(nb,),
        in_specs=[
            smem(lambda i, *_: (i, 0, 0)),
            smem(lambda i, *_: (jnp.minimum(i + 1, nb - 1), 0, 0)),
            pl.BlockSpec(memory_space=pl.ANY),
            pl.BlockSpec(memory_space=pl.ANY),
            pl.BlockSpec((None, None, 1, f2), lambda i, be, *_: (layer, be[i], 0, 0)),
            pl.BlockSpec(memory_space=pl.ANY),
            pl.BlockSpec((None, None, 1, d), lambda i, be, *_: (layer, be[i], 0, 0)),
        ],
        out_specs=pl.BlockSpec((tm, half), lambda i, *_: (i, 0)),
        scratch_shapes=[pltpu.VMEM((d, f2), F32), pltpu.VMEM((f, d), F32),
                        pltpu.VMEM((d, f2), BF16), pltpu.VMEM((f, d), BF16),
                        pltpu.VMEM((tm, half), U32), pltpu.VMEM((tm, half), U32),
                        pltpu.SemaphoreType.DMA((2,)), pltpu.SemaphoreType.DMA((2,))],
    )
    return pl.pallas_call(
        functools.partial(_moe_kernel, layer=layer),
        out_shape=jax.ShapeDtypeStruct((n_rows, half), U32),
        grid_spec=grid_spec,
        compiler_params=pltpu.CompilerParams(dimension_semantics=("arbitrary",),
                                             vmem_limit_bytes=VMEM_LIMIT),
        name="moe_experts",
    )(*plan, rsteps, rsteps, h2p, w_gu, b_gu.reshape(b_gu.shape[0], n_exp, 1, f2),
      w_down, b_down.reshape(b_down.shape[0], n_exp, 1, d))


def _ln2_kernel(dcur_ref, dnxt_ref, x_ref, yb_ref, tw_ref, mod_ref, g_ref, b_ref, o_ref,
                buf_even, buf_odd, sem, *, alpha):
    t = x_ref.shape[1]
    n_rows_step = TOP_K * t
    i = pl.program_id(0)

    def start_gather(d_ref, dst, s):
        for r in range(n_rows_step):
            pltpu.make_async_copy(yb_ref.at[pl.ds(d_ref[0, 0, r], 1)], dst.at[pl.ds(r, 1)],
                                  sem.at[s]).start()

    def wait_gather(dst, s):
        pltpu.make_async_copy(yb_ref.at[pl.ds(0, n_rows_step)], dst, sem.at[s]).wait()

    def combine(src):
        tw = tw_ref[0]
        y_hi = y_lo = None
        for k in range(TOP_K):
            hi, lo = _unpack_halves(src[k * t:(k + 1) * t, :])
            wk = tw[:, k:k + 1]
            y_hi = wk * hi if y_hi is None else y_hi + wk * hi
            y_lo = wk * lo if y_lo is None else y_lo + wk * lo
        y = jnp.concatenate([y_hi, y_lo], axis=1)
        o_ref[0] = _layer_norm(alpha * x_ref[0] + mod_ref[0, 5:6, :] * y, g_ref[...], b_ref[...])

    @pl.when(i == 0)
    def _():
        start_gather(dcur_ref, buf_even, 0)

    def step(cur, cur_sem, nxt, nxt_sem):
        wait_gather(cur, cur_sem)
        start_gather(dnxt_ref, nxt, nxt_sem)
        combine(cur)

        @pl.when(i == pl.num_programs(0) - 1)
        def _():
            wait_gather(nxt, nxt_sem)

    @pl.when(lax.rem(i, 2) == 0)
    def _():
        step(buf_even, 0, buf_odd, 1)

    @pl.when(lax.rem(i, 2) == 1)
    def _():
        step(buf_odd, 1, buf_even, 0)


def _ln2_call(x, yb, dest, top_w, mod, g, b, alpha):
    bsz, seq, d = x.shape
    t = min(seq, SEQ_TILE)
    spb = seq // t
    nst = bsz * spb
    dsteps = dest.reshape(nst, t, TOP_K).transpose(0, 2, 1).reshape(nst, 1, TOP_K * t)
    smem = lambda im: pl.BlockSpec((1, 1, TOP_K * t), im, memory_space=pltpu.SMEM)
    tile = lambda last: pl.BlockSpec((1, t, last), lambda i: (i // spb, i % spb, 0))
    return pl.pallas_call(
        functools.partial(_ln2_kernel, alpha=alpha),
        out_shape=jax.ShapeDtypeStruct((bsz, seq, d), F32),
        grid=(nst,),
        in_specs=[smem(lambda i: (i, 0, 0)),
                  smem(lambda i: (jnp.minimum(i + 1, nst - 1), 0, 0)),
                  tile(d),
                  pl.BlockSpec(memory_space=pl.ANY),
                  tile(TOP_K),
                  pl.BlockSpec((1, N_MOD, d), lambda i: (i // spb, 0, 0)),
                  pl.BlockSpec((1, d), lambda i: (0, 0)),
                  pl.BlockSpec((1, d), lambda i: (0, 0))],
        out_specs=tile(d),
        scratch_shapes=[pltpu.VMEM((TOP_K * t, d // 2), U32), pltpu.VMEM((TOP_K * t, d // 2), U32),
                        pltpu.SemaphoreType.DMA((2,))],
        compiler_params=pltpu.CompilerParams(dimension_semantics=("arbitrary",),
                                             vmem_limit_bytes=VMEM_LIMIT),
        name="moe_combine_ln",
    )(dsteps, dsteps, x, yb, top_w, mod, g.reshape(1, d), b.reshape(1, d))


def _router_kernel(lg_ref, o_ref, cnt_ref, carry, *, n_exp):
    tr = lg_ref.shape[0]

    @pl.when(pl.program_id(0) == 0)
    def _():
        carry[...] = jnp.zeros(carry.shape, carry.dtype)

    lane = lax.broadcasted_iota(jnp.int32, (tr, LANES), 1)
    lane_f = lane.astype(F32)
    neg = jnp.float32(-jnp.inf)
    lg = jnp.where(lane < n_exp, lg_ref[...], neg)
    sels, tops, ids = [], [], []
    for _ in range(TOP_K):
        m = jnp.max(lg, axis=-1, keepdims=True)
        idx = jnp.min(jnp.where(lg == m, lane_f, float(LANES)), axis=-1, keepdims=True)
        sel = lane_f == idx
        lg = jnp.where(sel, neg, lg)
        sels.append(sel)
        tops.append(m)
        ids.append(idx)
    exps = [jnp.exp(m - tops[0]) for m in tops]
    denom = exps[0]
    for e in exps[1:]:
        denom = denom + e
    onehot = jnp.where(sels[0], 1.0, 0.0)
    for sel in sels[1:]:
        onehot = onehot + jnp.where(sel, 1.0, 0.0)
    rr = lax.broadcasted_iota(jnp.int32, (tr, tr), 0)
    cc = lax.broadcasted_iota(jnp.int32, (tr, tr), 1)
    earlier = jnp.where(cc < rr, 1.0, 0.0).astype(BF16)
    prefix = jnp.dot(earlier, onehot.astype(BF16), preferred_element_type=F32) + carry[0:1, :]
    out = jnp.zeros((tr, LANES), F32)
    for k in range(TOP_K):
        rank = jnp.sum(jnp.where(sels[k], prefix, 0.0), axis=-1, keepdims=True)
        out = jnp.where(lane == k, exps[k] / denom, out)
        out = jnp.where(lane == TOP_K + k, ids[k], out)
        out = jnp.where(lane == 2 * TOP_K + k, rank, out)
    o_ref[...] = out
    carry[0:1, :] = carry[0:1, :] + jnp.sum(onehot, axis=0, keepdims=True)
    cnt_ref[...] = carry[...]


def _router_call(logits, n_exp):
    n = logits.shape[0]
    tr = min(n, ROUTER_ROWS)
    return pl.pallas_call(
        functools.partial(_router_kernel, n_exp=n_exp),
        out_shape=[jax.ShapeDtypeStruct((n, LANES), F32),
                   jax.ShapeDtypeStruct((SUBLANES, LANES), F32)],
        grid=(n // tr,),
        in_specs=[pl.BlockSpec((tr, LANES), lambda i: (i, 0))],
        out_specs=[pl.BlockSpec((tr, LANES), lambda i: (i, 0)),
                   pl.BlockSpec((SUBLANES, LANES), lambda i: (0, 0))],
        scratch_shapes=[pltpu.VMEM((SUBLANES, LANES), F32)],
        compiler_params=pltpu.CompilerParams(dimension_semantics=("arbitrary",),
                                             vmem_limit_bytes=VMEM_LIMIT),
        name="router_topk",
    )(logits)


def _route(logits, n_exp, tm):
    n = logits.shape[0]
    nk = n * TOP_K
    routed, cnt = _router_call(logits, n_exp)
    top_w = routed[:, 0:TOP_K]
    top_idx = routed[:, TOP_K:2 * TOP_K].astype(jnp.int32)
    rank = routed[:, 2 * TOP_K:3 * TOP_K].astype(jnp.int32)
    counts = cnt[0, :n_exp].astype(jnp.int32)
    padded = (counts + tm - 1) // tm * tm
    pend = jnp.cumsum(padded)
    pstart = pend - padded
    dest = pstart[top_idx] + rank
    n_rows = (nk + tm - 1) // tm * tm + n_exp * tm
    n_blocks = n_rows // tm
    n_used = (pend[-1] // tm).astype(jnp.int32)
    blk = jnp.arange(n_blocks, dtype=jnp.int32)
    blk_start = jnp.minimum(blk, n_used - 1) * tm
    block_e = jnp.minimum(jnp.sum(pend[None, :] <= blk_start[:, None], axis=1), n_exp - 1).astype(jnp.int32)
    tok = jnp.broadcast_to(jnp.arange(n, dtype=jnp.int32)[:, None], (n, TOP_K))
    _, tok_by_row = lax.sort((dest.reshape(-1), tok.reshape(-1)), num_keys=1)
    start = jnp.cumsum(counts) - counts
    rows = jnp.arange(n_rows, dtype=jnp.int32).reshape(n_blocks, tm)
    in_expert = rows - pstart[block_e][:, None]
    occupied = (in_expert < counts[block_e][:, None]) & (blk < n_used)[:, None]
    src = jnp.clip(start[block_e][:, None] + in_expert, 0, nk - 1)
    row_token = jnp.where(occupied, tok_by_row[src], rows % n).reshape(n_rows)
    first = ((blk < n_used) & ((blk == 0) | (block_e != jnp.roll(block_e, 1)))).astype(jnp.int32)
    ids = jnp.where(counts > 0, jnp.arange(n_exp, dtype=jnp.int32), n_exp)
    after = jnp.concatenate([lax.cummin(ids, axis=0, reverse=True)[1:], jnp.full((1,), n_exp, jnp.int32)])
    nxt = jnp.where(after < n_exp, after, -1)[block_e].astype(jnp.int32)
    plan = (block_e, first, nxt, n_used.reshape(1))
    return top_w, dest.astype(jnp.int32), row_token, plan


def kernel(x, c, w_ada, b_ada, ada_table, w_in, conv_a, dw_b_w, dw_b_b, ln_b_g, ln_b_b, ln_c_g, ln_c_b, sgu_w, sgu_b, w_out_a, w_out_b, w_out_c, w_gate, b_gate, w_o, ln1_g, ln1_b, w_router, b_router, w_gu, b_gu, w_down, b_down, ln2_g, ln2_b):
    bsz, seq, d = x.shape
    depth = w_in.shape[0]
    n_exp = w_router.shape[-1]
    w = conv_a.shape[-1]
    alpha = (2.0 * depth) ** 0.25
    n = bsz * seq

    mod_shared = _ada_call(c, w_ada, b_ada).reshape(bsz, N_MOD, d)
    w_cat = jnp.concatenate([w_in, w_gate], axis=-1).astype(BF16)
    p = dict(conv_a=conv_a, dw_b_w=dw_b_w, dw_b_b=dw_b_b, ln_b_g=ln_b_g, ln_b_b=ln_b_b,
             ln_c_g=ln_c_g, ln_c_b=ln_c_b, sgu_w=sgu_w, sgu_b=sgu_b,
             w_out_a=w_out_a.astype(BF16), w_out_b=w_out_b.astype(BF16),
             w_out_c=w_out_c.astype(BF16), b_gate=b_gate, w_o=w_o.astype(BF16),
             ln1_g=ln1_g, ln1_b=ln1_b, w_router=w_router, b_router=b_router)
    tm = MOE_ROWS

    for l in range(depth):
        mod = mod_shared + ada_table[l][None]
        proj = _proj_call(x, mod, w_cat, l, w)
        x, h2, logits = _mixer_call(proj, x, mod, p, l, alpha)

        top_w, dest, row_token, plan = _route(logits.reshape(n, -1), n_exp, tm)
        yb = _moe_call(plan, row_token, h2.reshape(n, d // 2), w_gu, b_gu, w_down, b_down, l, tm)
        x = _ln2_call(x, yb, dest, top_w.reshape(bsz, seq, TOP_K), mod, ln2_g[l], ln2_b[l], alpha)
    return x
```

```python
import functools

import jax
import jax.numpy as jnp
from jax import lax
from jax.experimental import pallas as pl
from jax.experimental.pallas import tpu as pltpu

F32 = jnp.float32
BF16 = jnp.bfloat16
U32 = jnp.uint32

CHUNK = 64
SGU_BLOCK = 128
TOP_K = 4
SWIGLU_LIMIT = 7.0
SWIGLU_ALPHA = 1.702
LN_EPS = 1e-5
N_MOD = 6

LANES = 128
SUBLANES = 8
CONV_ROWS = 64
CONV_COLS = 2 * LANES
CONV_HALO = 32
ROUTER_LANES = LANES
VMEM_LIMIT = 56 * 1024 * 1024

SEQ_TILE = 256
PROJ_ROWS = 1024
MOE_ROWS = 512
ROUTER_ROWS = 512
N_DMA_PRIORITIES = 2


def _layer_norm(x, g, b):
    mu = jnp.mean(x, axis=-1, keepdims=True)
    xc = x - mu
    var = jnp.mean(xc * xc, axis=-1, keepdims=True)
    return xc * lax.rsqrt(var + LN_EPS) * g + b


def _gelu(x):
    c = 0.7978845608028654
    return 0.5 * x * (1.0 + jnp.tanh(c * (x + 0.044715 * (x * x * x))))


def _sigmoid(x):
    return 1.0 / (1.0 + jnp.exp(-x))


def _pack_halves(y):
    half = y.shape[1] // 2
    hi = lax.bitcast_convert_type(y[:, :half].astype(BF16).astype(F32), U32)
    lo = lax.bitcast_convert_type(y[:, half:].astype(BF16).astype(F32), U32)
    return hi | (lo >> 16)


def _unpack_halves(p):
    hi = lax.bitcast_convert_type(p & jnp.uint32(0xFFFF0000), F32)
    lo = lax.bitcast_convert_type(p << 16, F32)
    return hi, lo


def _ada_kernel(c_ref, w_ref, b_ref, o_ref):
    c = c_ref[...]
    a = c * _sigmoid(c)
    o_ref[...] = jnp.dot(a, w_ref[...], preferred_element_type=F32,
                         precision=lax.Precision.HIGHEST) + b_ref[...]


def _ada_call(c, w_ada, b_ada):
    bsz, d = c.shape
    nc = w_ada.shape[1]
    bn = min(nc, 1024)
    return pl.pallas_call(
        _ada_kernel,
        out_shape=jax.ShapeDtypeStruct((bsz, nc), F32),
        grid=(nc // bn,),
        in_specs=[pl.BlockSpec((bsz, d), lambda j: (0, 0)),
                  pl.BlockSpec((d, bn), lambda j: (0, j)),
                  pl.BlockSpec((1, bn), lambda j: (0, j))],
        out_specs=pl.BlockSpec((bsz, bn), lambda j: (0, j)),
        compiler_params=pltpu.CompilerParams(dimension_semantics=("parallel",),
                                             vmem_limit_bytes=VMEM_LIMIT),
        name="ada_proj",
    )(c, w_ada, b_ada.reshape(1, nc))


def _proj_kernel(x_ref, mod_ref, w_ref, o_ref, h_ref):
    @pl.when(pl.program_id(2) == 0)
    def _():
        sh = mod_ref[0, 0:1, :]
        sc = mod_ref[0, 1:2, :]
        h_ref[...] = (x_ref[0] * (1.0 + sc) + sh).astype(BF16)

    o_ref[0] = jnp.dot(h_ref[...], w_ref[...], preferred_element_type=F32).astype(o_ref.dtype)


def _proj_call(x, mod, w_cat, layer, bn):
    bsz, seq, d = x.shape
    nc = w_cat.shape[-1]
    bm = min(seq, PROJ_ROWS)
    return pl.pallas_call(
        _proj_kernel,
        out_shape=jax.ShapeDtypeStruct((bsz, seq, nc), BF16),
        grid=(bsz, seq // bm, nc // bn),
        in_specs=[pl.BlockSpec((1, bm, d), lambda b, i, j: (b, i, 0)),
                  pl.BlockSpec((1, N_MOD, d), lambda b, i, j: (b, 0, 0)),
                  pl.BlockSpec((None, d, bn), lambda b, i, j: (layer, 0, j))],
        out_specs=pl.BlockSpec((1, bm, bn), lambda b, i, j: (b, i, j)),
        scratch_shapes=[pltpu.VMEM((bm, d), BF16)],
        compiler_params=pltpu.CompilerParams(
            dimension_semantics=("parallel", "parallel", "arbitrary"),
            vmem_limit_bytes=VMEM_LIMIT),
        name="mixer_in_proj",
    )(x, mod, w_cat)


def _mixer_kernel(proj_ref, halo_ref, x_ref, mod_ref, ca_ref, dww_ref, dwb_ref,
                  lnbg_ref, lnbb_ref, lncg_ref, lncb_ref, sw_ref, sb_ref,
                  woa_ref, wob_ref, woc_ref, bg_ref, wo_ref, ln1g_ref, ln1b_ref,
                  wr_ref, br_ref,
                  xo_ref, h2_ref, lg_ref,
                  abuf, bbuf, cbuf, sbuf, *, alpha):
    t = x_ref.shape[1]
    d = x_ref.shape[2]
    w = ca_ref.shape[1]
    n_heads = sw_ref.shape[0]
    hd = w // n_heads
    ka = ca_ref.shape[0]
    kb = dww_ref.shape[0]
    hal = CONV_HALO

    def pf(k):
        return proj_ref[0, :, k * w:(k + 1) * w].astype(F32)

    def hf(k):
        return halo_ref[0, :, k * w:(k + 1) * w].astype(F32)

    keep = jnp.where(pl.program_id(1) > 0, 1.0, 0.0).astype(F32)

    abuf[0:hal, :] = hf(0) * hf(2) * keep
    abuf[hal:hal + t, :] = pf(0) * pf(2)
    conv_a = ca_ref[0:1, :] * abuf[hal - ka + 1:hal - ka + 1 + t, :]
    for k in range(1, ka):
        conv_a = conv_a + ca_ref[k:k + 1, :] * abuf[hal - ka + 1 + k:hal - ka + 1 + k + t, :]
    y_a = (pf(1) * conv_a).astype(BF16)
    o_a = jnp.dot(y_a, woa_ref[...], preferred_element_type=F32)

    bbuf[0:hal, :] = hf(3) * _sigmoid(hf(4)) * keep
    bbuf[hal:hal + t, :] = pf(3) * _sigmoid(pf(4))
    rows = CONV_ROWS
    cols = sbuf.shape[2]
    span = sbuf.shape[1]
    for c0 in range(0, w, cols):
        for r in range(1, SUBLANES):
            sbuf[r - 1] = bbuf[r:r + span, c0:c0 + cols]
        for r0 in range(0, t, rows):
            acc = None
            for k in range(kb):
                q, r = divmod(hal - kb + 1 + k, SUBLANES)
                lo = r0 + SUBLANES * q
                if r == 0:
                    src = bbuf[lo:lo + rows, c0:c0 + cols]
                else:
                    src = sbuf[r - 1, lo:lo + rows, :]
                term = dww_ref[k:k + 1, c0:c0 + cols] * src
                acc = term if acc is None else acc + term
            cbuf[r0:r0 + rows, c0:c0 + cols] = acc
    y_b = _layer_norm(cbuf[...] + dwb_ref[...], lnbg_ref[...], lnbb_ref[...])
    y_b = (y_b * _sigmoid(y_b)).astype(BF16)
    o_b = jnp.dot(y_b, wob_ref[...], preferred_element_type=F32)

    v = _layer_norm(_gelu(pf(6)), lncg_ref[...], lncb_ref[...]).astype(BF16)
    ri = lax.broadcasted_iota(jnp.int32, (SGU_BLOCK, SGU_BLOCK), 0) // CHUNK
    ci = lax.broadcasted_iota(jnp.int32, (SGU_BLOCK, SGU_BLOCK), 1) // CHUNK
    causal = ri >= ci
    for g in range(n_heads):
        wm = jnp.where(causal, sw_ref[g], 0.0).astype(BF16)
        for blk in range(t // SGU_BLOCK):
            r0 = blk * SGU_BLOCK
            cbuf[r0:r0 + SGU_BLOCK, g * hd:(g + 1) * hd] = (
                jnp.dot(wm, v[r0:r0 + SGU_BLOCK, g * hd:(g + 1) * hd], preferred_element_type=F32)
                + sb_ref[:, g * hd:(g + 1) * hd])
    y_c = (_gelu(pf(5)) * cbuf[...]).astype(BF16)
    o_c = jnp.dot(y_c, woc_ref[...], preferred_element_type=F32)

    g0 = 7 * w

    def gate(k):
        z = proj_ref[0, :, g0 + k * d:g0 + (k + 1) * d].astype(F32) + bg_ref[:, k * d:(k + 1) * d]
        return _sigmoid(z)

    merged = (gate(0) * o_a + gate(1) * o_b + gate(2) * o_c).astype(BF16)
    mix = jnp.dot(merged, wo_ref[...], preferred_element_type=F32)
    xn = _layer_norm(alpha * x_ref[0] + mod_ref[0, 2:3, :] * mix, ln1g_ref[...], ln1b_ref[...])
    xo_ref[0] = xn

    h2 = xn * (1.0 + mod_ref[0, 4:5, :]) + mod_ref[0, 3:4, :]
    h2_hi = h2.astype(BF16)
    h2_ref[0] = _pack_halves(h2)
    h2_lo = (h2 - h2_hi.astype(F32)).astype(BF16)
    p_hi = jnp.dot(h2_hi, wr_ref[...], preferred_element_type=F32)
    p_lo = jnp.dot(h2_lo, wr_ref[:, 0:ROUTER_LANES], preferred_element_type=F32)
    lg_ref[0] = p_hi[:, 0:ROUTER_LANES] + p_hi[:, ROUTER_LANES:] + p_lo + br_ref[...]


def _const_spec(shape, index):
    return pl.BlockSpec(shape, index, pipeline_mode=pl.Buffered(1))


def _mixer_call(proj, x, mod, p, layer, alpha):
    bsz, seq, d = x.shape
    w = p["conv_a"].shape[-1]
    nc = proj.shape[-1]
    t = min(seq, SEQ_TILE)
    hal = CONV_HALO
    hb = t // hal
    n_heads = p["sgu_w"].shape[1]
    row = lambda a: a[layer].reshape(1, -1)
    grid = (bsz, seq // t)
    cst = lambda *shape: _const_spec(shape, lambda b, s: (0,) * len(shape))
    lyr = lambda *shape: _const_spec((None,) + shape, lambda b, s: (layer,) + (0,) * len(shape))
    in_specs = [
        pl.BlockSpec((1, t, nc), lambda b, s: (b, s, 0)),
        pl.BlockSpec((1, hal, 7 * w), lambda b, s: (b, jnp.maximum(s * hb - 1, 0), 0)),
        pl.BlockSpec((1, t, d), lambda b, s: (b, s, 0)),
        pl.BlockSpec((1, N_MOD, d), lambda b, s: (b, 0, 0)),
        lyr(p["conv_a"].shape[1], w),
        lyr(p["dw_b_w"].shape[1], w),
        cst(1, w), cst(1, w), cst(1, w), cst(1, w), cst(1, w),
        lyr(n_heads, SGU_BLOCK, SGU_BLOCK),
        cst(SGU_BLOCK, w),
        lyr(w, d), lyr(w, d), lyr(w, d),
        cst(1, 3 * d),
        lyr(d, d),
        cst(1, d), cst(1, d),
        cst(d, 2 * ROUTER_LANES), cst(1, ROUTER_LANES),
    ]
    out_specs = [
        pl.BlockSpec((1, t, d), lambda b, s: (b, s, 0)),
        pl.BlockSpec((1, t, d // 2), lambda b, s: (b, s, 0)),
        pl.BlockSpec((1, t, ROUTER_LANES), lambda b, s: (b, s, 0)),
    ]
    out_shape = [
        jax.ShapeDtypeStruct((bsz, seq, d), F32),
        jax.ShapeDtypeStruct((bsz, seq, d // 2), U32),
        jax.ShapeDtypeStruct((bsz, seq, ROUTER_LANES), F32),
    ]
    n_exp = p["w_router"].shape[-1]
    wr = jnp.pad(p["w_router"][layer], ((0, 0), (0, ROUTER_LANES - n_exp)))
    wr_hi = wr.astype(BF16)
    wr = jnp.concatenate([wr_hi, (wr - wr_hi.astype(F32)).astype(BF16)], axis=1)
    br = jnp.pad(p["b_router"][layer], (0, ROUTER_LANES - n_exp)).reshape(1, -1)
    sb = jnp.repeat(p["sgu_b"][layer].T, w // n_heads, axis=1)
    return pl.pallas_call(
        functools.partial(_mixer_kernel, alpha=alpha),
        out_shape=out_shape,
        grid=grid,
        in_specs=in_specs,
        out_specs=out_specs,
        scratch_shapes=[pltpu.VMEM((t + hal, w), F32), pltpu.VMEM((t + hal, w), F32),
                        pltpu.VMEM((t, w), F32),
                        pltpu.VMEM((SUBLANES - 1, t + hal - SUBLANES, min(w, CONV_COLS)), F32)],
        compiler_params=pltpu.CompilerParams(dimension_semantics=("parallel", "parallel"),
                                             vmem_limit_bytes=VMEM_LIMIT),
        name="mixer",
    )(proj, proj, x, mod, p["conv_a"], p["dw_b_w"], row(p["dw_b_b"]),
      row(p["ln_b_g"]), row(p["ln_b_b"]), row(p["ln_c_g"]), row(p["ln_c_b"]),
      p["sgu_w"], sb, p["w_out_a"], p["w_out_b"], p["w_out_c"], row(p["b_gate"]),
      p["w_o"], row(p["ln1_g"]), row(p["ln1_b"]), wr, br)


def _moe_kernel(be_ref, first_ref, nxt_ref, nb_ref, rcur_ref, rnxt_ref, h_hbm, wgu_hbm, bgu_ref,
                wd_hbm, bd_ref, o_ref,
                gu_stage, d_stage, wgu_bf, wd_bf, x_even, x_odd, gu_buf, wsem, xsem, *, layer):
    f = wd_bf.shape[0]
    tm, half = x_even.shape
    i = pl.program_id(0)

    def weight_copies(expert):
        return (pltpu.make_async_copy(wgu_hbm.at[layer, expert], gu_stage, wsem.at[0]),
                pltpu.make_async_copy(wd_hbm.at[layer, expert], d_stage, wsem.at[1]))

    def row_copy(r_ref, r, dst, s):
        return pltpu.make_async_copy(h_hbm.at[pl.ds(r_ref[0, 0, r], 1)], dst.at[pl.ds(r, 1)],
                                     xsem.at[s])

    def start_rows_inline(r_ref, dst, s):
        for r in range(tm):
            row_copy(r_ref, r, dst, s).start()

    def start_rows_loop(r_ref, dst, s):
        def body(r, carry):
            row_copy(r_ref, r, dst, s).start()
            return carry
        lax.fori_loop(0, tm, body, 0)

    def wait_rows(dst, s):
        pltpu.make_async_copy(h_hbm.at[pl.ds(0, tm)], dst, xsem.at[s]).wait()

    @pl.when(i == 0)
    def _():
        for cp in weight_copies(be_ref[0]):
            cp.start(priority=1)
        start_rows_loop(rcur_ref, x_even, 0)

    @pl.when(first_ref[i] == 1)
    def _():
        for cp in weight_copies(be_ref[i]):
            cp.wait()
        wgu_bf[...] = gu_stage[...].astype(BF16)
        wd_bf[...] = d_stage[...].astype(BF16)

        @pl.when(nxt_ref[i] >= 0)
        def _():
            for cp in weight_copies(nxt_ref[i]):
                cp.start(priority=1)

    def step(cur, cur_sem, nxt, nxt_sem):
        wait_rows(cur, cur_sem)

        @pl.when(i < nb_ref[0])
        def _():
            start_rows_inline(rnxt_ref, nxt, nxt_sem)
            hi, _ = _unpack_halves(cur[...])
            gu_buf[...] = jnp.dot(hi.astype(BF16), wgu_bf[0:half, :],
                                  preferred_element_type=F32) + bgu_ref[...]

        @pl.when(i < nb_ref[0])
        def _():
            _, lo = _unpack_halves(cur[...])
            gu_buf[...] += jnp.dot(lo.astype(BF16), wgu_bf[half:, :], preferred_element_type=F32)

        @pl.when(i < nb_ref[0])
        def _():
            gu = gu_buf[...]
            gate = jnp.minimum(gu[:, :f], SWIGLU_LIMIT)
            up = jnp.clip(gu[:, f:], -SWIGLU_LIMIT, SWIGLU_LIMIT)
            act = (up + 1.0) * (gate * _sigmoid(SWIGLU_ALPHA * gate))
            y = jnp.dot(act.astype(BF16), wd_bf[...], preferred_element_type=F32) + bd_ref[...]
            o_ref[...] = _pack_halves(y)

        @pl.when(i >= nb_ref[0])
        def _():
            start_rows_loop(rnxt_ref, nxt, nxt_sem)
            o_ref[...] = jnp.zeros(o_ref.shape, o_ref.dtype)

        @pl.when(i == pl.num_programs(0) - 1)
        def _():
            wait_rows(nxt, nxt_sem)

    @pl.when(lax.rem(i, 2) == 0)
    def _():
        step(x_even, 0, x_odd, 1)

    @pl.when(lax.rem(i, 2) == 1)
    def _():
        step(x_odd, 1, x_even, 0)


def _moe_call(plan, row_token, h2p, w_gu, b_gu, w_down, b_down, layer, tm):
    half = h2p.shape[1]
    d = 2 * half
    n_rows = row_token.shape[0]
    nb = n_rows // tm
    n_exp, _, f2 = w_gu.shape[1:]
    f = f2 // 2
    rsteps = row_token.reshape(nb, 1, tm)
    smem = lambda im: pl.BlockSpec((1, 1, tm), im, memory_space=pltpu.SMEM)
    grid_spec = pltpu.PrefetchScalarGridSpec(
        num_scalar_prefetch=4,
        grid=(nb,),
        in_specs=[
            smem(lambda i, *_: (i, 0, 0)),
            smem(lambda i, *_: (jnp.minimum(i + 1, nb - 1), 0, 0)),
            pl.BlockSpec(memory_space=pl.ANY),
            pl.BlockSpec(memory_space=pl.ANY),
            pl.BlockSpec((None, None, 1, f2), lambda i, be, *_: (layer, be[i], 0, 0)),
            pl.BlockSpec(memory_space=pl.ANY),
            pl.BlockSpec((None, None, 1, d), lambda i, be, *_: (layer, be[i], 0, 0)),
        ],
        out_specs=pl.BlockSpec((tm, half), lambda i, *_: (i, 0)),
        scratch_shapes=[pltpu.VMEM((d, f2), F32), pltpu.VMEM((f, d), F32),
                        pltpu.VMEM((d, f2), BF16), pltpu.VMEM((f, d), BF16),
                        pltpu.VMEM((tm, half), U32), pltpu.VMEM((tm, half), U32),
                        pltpu.VMEM((tm, f2), F32),
                        pltpu.SemaphoreType.DMA((2,)), pltpu.SemaphoreType.DMA((2,))],
    )
    return pl.pallas_call(
        functools.partial(_moe_kernel, layer=layer),
        out_shape=jax.ShapeDtypeStruct((n_rows, half), U32),
        grid_spec=grid_spec,
        compiler_params=pltpu.CompilerParams(dimension_semantics=("arbitrary",),
                                             vmem_limit_bytes=VMEM_LIMIT),
        name="moe_experts",
    )(*plan, rsteps, rsteps, h2p, w_gu, b_gu.reshape(b_gu.shape[0], n_exp, 1, f2),
      w_down, b_down.reshape(b_down.shape[0], n_exp, 1, d))


def _ln2_kernel(dcur_ref, dnxt_ref, x_ref, yb_ref, tw_ref, mod_ref, g_ref, b_ref, o_ref,
                buf_even, buf_odd, sem, *, alpha):
    t = x_ref.shape[1]
    n_rows_step = TOP_K * t
    i = pl.program_id(0)

    def start_gather(d_ref, dst, s):
        for r in range(n_rows_step):
            pltpu.make_async_copy(yb_ref.at[pl.ds(d_ref[0, 0, r], 1)], dst.at[pl.ds(r, 1)],
                                  sem.at[s]).start(priority=r % N_DMA_PRIORITIES)

    def wait_gather(dst, s):
        pltpu.make_async_copy(yb_ref.at[pl.ds(0, n_rows_step)], dst, sem.at[s]).wait()

    def combine(src):
        tw = tw_ref[0]
        y_hi = y_lo = None
        for k in range(TOP_K):
            hi, lo = _unpack_halves(src[k * t:(k + 1) * t, :])
            wk = tw[:, k:k + 1]
            y_hi = wk * hi if y_hi is None else y_hi + wk * hi
            y_lo = wk * lo if y_lo is None else y_lo + wk * lo
        y = jnp.concatenate([y_hi, y_lo], axis=1)
        o_ref[0] = _layer_norm(alpha * x_ref[0] + mod_ref[0, 5:6, :] * y, g_ref[...], b_ref[...])

    @pl.when(i == 0)
    def _():
        start_gather(dcur_ref, buf_even, 0)

    def step(cur, cur_sem, nxt, nxt_sem):
        wait_gather(cur, cur_sem)
        start_gather(dnxt_ref, nxt, nxt_sem)
        combine(cur)

        @pl.when(i == pl.num_programs(0) - 1)
        def _():
            wait_gather(nxt, nxt_sem)

    @pl.when(lax.rem(i, 2) == 0)
    def _():
        step(buf_even, 0, buf_odd, 1)

    @pl.when(lax.rem(i, 2) == 1)
    def _():
        step(buf_odd, 1, buf_even, 0)


def _ln2_call(x, yb, dest, top_w, mod, g, b, alpha):
    bsz, seq, d = x.shape
    t = min(seq, SEQ_TILE)
    spb = seq // t
    nst = bsz * spb
    dsteps = dest.reshape(nst, t, TOP_K).transpose(0, 2, 1).reshape(nst, 1, TOP_K * t)
    smem = lambda im: pl.BlockSpec((1, 1, TOP_K * t), im, memory_space=pltpu.SMEM)
    tile = lambda last: pl.BlockSpec((1, t, last), lambda i: (i // spb, i % spb, 0))
    return pl.pallas_call(
        functools.partial(_ln2_kernel, alpha=alpha),
        out_shape=jax.ShapeDtypeStruct((bsz, seq, d), F32),
        grid=(nst,),
        in_specs=[smem(lambda i: (i, 0, 0)),
                  smem(lambda i: (jnp.minimum(i + 1, nst - 1), 0, 0)),
                  tile(d),
                  pl.BlockSpec(memory_space=pl.ANY),
                  tile(TOP_K),
                  pl.BlockSpec((1, N_MOD, d), lambda i: (i // spb, 0, 0)),
                  pl.BlockSpec((1, d), lambda i: (0, 0)),
                  pl.BlockSpec((1, d), lambda i: (0, 0))],
        out_specs=tile(d),
        scratch_shapes=[pltpu.VMEM((TOP_K * t, d // 2), U32), pltpu.VMEM((TOP_K * t, d // 2), U32),
                        pltpu.SemaphoreType.DMA((2,))],
        compiler_params=pltpu.CompilerParams(dimension_semantics=("arbitrary",),
                                             vmem_limit_bytes=VMEM_LIMIT),
        name="moe_combine_ln",
    )(dsteps, dsteps, x, yb, top_w, mod, g.reshape(1, d), b.reshape(1, d))


def _router_kernel(lg_ref, o_ref, cnt_ref, carry, *, n_exp):
    tr = lg_ref.shape[0]

    @pl.when(pl.program_id(0) == 0)
    def _():
        carry[...] = jnp.zeros(carry.shape, carry.dtype)

    lane = lax.broadcasted_iota(jnp.int32, (tr, LANES), 1)
    lane_f = lane.astype(F32)
    neg = jnp.float32(-jnp.inf)
    lg = jnp.where(lane < n_exp, lg_ref[...], neg)
    sels, tops, ids = [], [], []
    for _ in range(TOP_K):
        m = jnp.max(lg, axis=-1, keepdims=True)
        idx = jnp.min(jnp.where(lg == m, lane_f, float(LANES)), axis=-1, keepdims=True)
        sel = lane_f == idx
        lg = jnp.where(sel, neg, lg)
        sels.append(sel)
        tops.append(m)
        ids.append(idx)
    exps = [jnp.exp(m - tops[0]) for m in tops]
    denom = exps[0]
    for e in exps[1:]:
        denom = denom + e
    onehot = jnp.where(sels[0], 1.0, 0.0)
    for sel in sels[1:]:
        onehot = onehot + jnp.where(sel, 1.0, 0.0)
    rr = lax.broadcasted_iota(jnp.int32, (tr, tr), 0)
    cc = lax.broadcasted_iota(jnp.int32, (tr, tr), 1)
    earlier = jnp.where(cc < rr, 1.0, 0.0).astype(BF16)
    prefix = jnp.dot(earlier, onehot.astype(BF16), preferred_element_type=F32) + carry[0:1, :]
    out = jnp.zeros((tr, LANES), F32)
    for k in range(TOP_K):
        rank = jnp.sum(jnp.where(sels[k], prefix, 0.0), axis=-1, keepdims=True)
        out = jnp.where(lane == k, exps[k] / denom, out)
        out = jnp.where(lane == TOP_K + k, ids[k], out)
        out = jnp.where(lane == 2 * TOP_K + k, rank, out)
    o_ref[...] = out
    carry[0:1, :] = carry[0:1, :] + jnp.sum(onehot, axis=0, keepdims=True)
    cnt_ref[...] = carry[...]


def _router_call(logits, n_exp):
    n = logits.shape[0]
    tr = min(n, ROUTER_ROWS)
    return pl.pallas_call(
        functools.partial(_router_kernel, n_exp=n_exp),
        out_shape=[jax.ShapeDtypeStruct((n, LANES), F32),
                   jax.ShapeDtypeStruct((SUBLANES, LANES), F32)],
        grid=(n // tr,),
        in_specs=[pl.BlockSpec((tr, LANES), lambda i: (i, 0))],
        out_specs=[pl.BlockSpec((tr, LANES), lambda i: (i, 0)),
                   pl.BlockSpec((SUBLANES, LANES), lambda i: (0, 0))],
        scratch_shapes=[pltpu.VMEM((SUBLANES, LANES), F32)],
        compiler_params=pltpu.CompilerParams(dimension_semantics=("arbitrary",),
                                             vmem_limit_bytes=VMEM_LIMIT),
        name="router_topk",
    )(logits)


def _route(logits, n_exp, tm):
    n = logits.shape[0]
    nk = n * TOP_K
    routed, cnt = _router_call(logits, n_exp)
    top_w = routed[:, 0:TOP_K]
    top_idx = routed[:, TOP_K:2 * TOP_K].astype(jnp.int32)
    rank = routed[:, 2 * TOP_K:3 * TOP_K].astype(jnp.int32)
    counts = cnt[0, :n_exp].astype(jnp.int32)
    padded = (counts + tm - 1) // tm * tm
    pend = jnp.cumsum(padded)
    pstart = pend - padded
    dest = pstart[top_idx] + rank
    n_rows = (nk + tm - 1) // tm * tm + n_exp * tm
    n_blocks = n_rows // tm
    n_used = (pend[-1] // tm).astype(jnp.int32)
    blk = jnp.arange(n_blocks, dtype=jnp.int32)
    blk_start = jnp.minimum(blk, n_used - 1) * tm
    block_e = jnp.minimum(jnp.sum(pend[None, :] <= blk_start[:, None], axis=1), n_exp - 1).astype(jnp.int32)
    tok = jnp.broadcast_to(jnp.arange(n, dtype=jnp.int32)[:, None], (n, TOP_K))
    _, tok_by_row = lax.sort((dest.reshape(-1), tok.reshape(-1)), num_keys=1)
    start = jnp.cumsum(counts) - counts
    rows = jnp.arange(n_rows, dtype=jnp.int32).reshape(n_blocks, tm)
    in_expert = rows - pstart[block_e][:, None]
    occupied = (in_expert < counts[block_e][:, None]) & (blk < n_used)[:, None]
    src = jnp.clip(start[block_e][:, None] + in_expert, 0, nk - 1)
    row_token = jnp.where(occupied, tok_by_row[src], rows % n).reshape(n_rows)
    first = ((blk < n_used) & ((blk == 0) | (block_e != jnp.roll(block_e, 1)))).astype(jnp.int32)
    ids = jnp.where(counts > 0, jnp.arange(n_exp, dtype=jnp.int32), n_exp)
    after = jnp.concatenate([lax.cummin(ids, axis=0, reverse=True)[1:], jnp.full((1,), n_exp, jnp.int32)])
    nxt = jnp.where(after < n_exp, after, -1)[block_e].astype(jnp.int32)
    plan = (block_e, first, nxt, n_used.reshape(1))
    return top_w, dest.astype(jnp.int32), row_token, plan


def kernel(x, c, w_ada, b_ada, ada_table, w_in, conv_a, dw_b_w, dw_b_b, ln_b_g, ln_b_b, ln_c_g, ln_c_b, sgu_w, sgu_b, w_out_a, w_out_b, w_out_c, w_gate, b_gate, w_o, ln1_g, ln1_b, w_router, b_router, w_gu, b_gu, w_down, b_down, ln2_g, ln2_b):
    bsz, seq, d = x.shape
    depth = w_in.shape[0]
    n_exp = w_router.shape[-1]
    w = conv_a.shape[-1]
    alpha = (2.0 * depth) ** 0.25
    n = bsz * seq

    mod_shared = _ada_call(c, w_ada, b_ada).reshape(bsz, N_MOD, d)
    w_cat = jnp.concatenate([w_in, w_gate], axis=-1).astype(BF16)
    p = dict(conv_a=conv_a, dw_b_w=dw_b_w, dw_b_b=dw_b_b, ln_b_g=ln_b_g, ln_b_b=ln_b_b,
             ln_c_g=ln_c_g, ln_c_b=ln_c_b, sgu_w=sgu_w, sgu_b=sgu_b,
             w_out_a=w_out_a.astype(BF16), w_out_b=w_out_b.astype(BF16),
             w_out_c=w_out_c.astype(BF16), b_gate=b_gate, w_o=w_o.astype(BF16),
             ln1_g=ln1_g, ln1_b=ln1_b, w_router=w_router, b_router=b_router)
    tm = MOE_ROWS

    for l in range(depth):
        mod = mod_shared + ada_table[l][None]
        proj = _proj_call(x, mod, w_cat, l, w)
        x, h2, logits = _mixer_call(proj, x, mod, p, l, alpha)

        top_w, dest, row_token, plan = _route(logits.reshape(n, -1), n_exp, tm)
        yb = _moe_call(plan, row_token, h2.reshape(n, d // 2), w_gu, b_gu, w_down, b_down, l, tm)
        x = _ln2_call(x, yb, dest, top_w.reshape(bsz, seq, TOP_K), mod, ln2_g[l], ln2_b[l], alpha)
    return x
```

```python
import functools

import jax
import jax.numpy as jnp
from jax import lax
from jax.experimental import pallas as pl
from jax.experimental.pallas import tpu as pltpu

F32 = jnp.float32
BF16 = jnp.bfloat16
U32 = jnp.uint32

CHUNK = 64
SGU_BLOCK = 128
TOP_K = 4
SWIGLU_LIMIT = 7.0
SWIGLU_ALPHA = 1.702
LN_EPS = 1e-5
N_MOD = 6

LANES = 128
SUBLANES = 8
CONV_ROWS = 128
CONV_COLS = LANES
CONV_HALO = 32
ROUTER_LANES = LANES
VMEM_LIMIT = 56 * 1024 * 1024

SEQ_TILE = 256
PROJ_ROWS = 1024
MOE_ROWS = 512
ROUTER_ROWS = 512
N_DMA_PRIORITIES = 2


def _layer_norm(x, g, b):
    mu = jnp.mean(x, axis=-1, keepdims=True)
    xc = x - mu
    var = jnp.mean(xc * xc, axis=-1, keepdims=True)
    return xc * lax.rsqrt(var + LN_EPS) * g + b


def _gelu(x):
    c = 0.7978845608028654
    return 0.5 * x * (1.0 + jnp.tanh(c * (x + 0.044715 * (x * x * x))))


def _sigmoid(x):
    return 1.0 / (1.0 + jnp.exp(-x))


def _pack_halves(y):
    half = y.shape[1] // 2
    hi = lax.bitcast_convert_type(y[:, :half].astype(BF16).astype(F32), U32)
    lo = lax.bitcast_convert_type(y[:, half:].astype(BF16).astype(F32), U32)
    return hi | (lo >> 16)


def _unpack_halves(p):
    hi = lax.bitcast_convert_type(p & jnp.uint32(0xFFFF0000), F32)
    lo = lax.bitcast_convert_type(p << 16, F32)
    return hi, lo


def _ada_kernel(c_ref, w_ref, b_ref, o_ref):
    c = c_ref[...]
    a = c * _sigmoid(c)
    o_ref[...] = jnp.dot(a, w_ref[...], preferred_element_type=F32,
                         precision=lax.Precision.HIGHEST) + b_ref[...]


def _ada_call(c, w_ada, b_ada):
    bsz, d = c.shape
    nc = w_ada.shape[1]
    bn = min(nc, 1024)
    return pl.pallas_call(
        _ada_kernel,
        out_shape=jax.ShapeDtypeStruct((bsz, nc), F32),
        grid=(nc // bn,),
        in_specs=[pl.BlockSpec((bsz, d), lambda j: (0, 0)),
                  pl.BlockSpec((d, bn), lambda j: (0, j)),
                  pl.BlockSpec((1, bn), lambda j: (0, j))],
        out_specs=pl.BlockSpec((bsz, bn), lambda j: (0, j)),
        compiler_params=pltpu.CompilerParams(dimension_semantics=("parallel",),
                                             vmem_limit_bytes=VMEM_LIMIT),
        name="ada_proj",
    )(c, w_ada, b_ada.reshape(1, nc))


def _proj_kernel(x_ref, mod_ref, win_ref, wg_ref, o_ref, h_ref, *, n_in):
    j = pl.program_id(2)

    @pl.when(j == 0)
    def _():
        sh = mod_ref[0, 0:1, :]
        sc = mod_ref[0, 1:2, :]
        h_ref[...] = (x_ref[0] * (1.0 + sc) + sh).astype(BF16)

    @pl.when(j < n_in)
    def _():
        o_ref[0] = jnp.dot(h_ref[...], win_ref[...], preferred_element_type=F32).astype(o_ref.dtype)

    @pl.when(j >= n_in)
    def _():
        o_ref[0] = jnp.dot(h_ref[...], wg_ref[...], preferred_element_type=F32).astype(o_ref.dtype)


def _proj_call(x, mod, w_in, w_gate, layer, bn):
    bsz, seq, d = x.shape
    n_in = w_in.shape[-1] // bn
    n_gate = w_gate.shape[-1] // bn
    nc = (n_in + n_gate) * bn
    bm = min(seq, PROJ_ROWS)
    return pl.pallas_call(
        functools.partial(_proj_kernel, n_in=n_in),
        out_shape=jax.ShapeDtypeStruct((bsz, seq, nc), BF16),
        grid=(bsz, seq // bm, n_in + n_gate),
        in_specs=[pl.BlockSpec((1, bm, d), lambda b, i, j: (b, i, 0)),
                  pl.BlockSpec((1, N_MOD, d), lambda b, i, j: (b, 0, 0)),
                  pl.BlockSpec((None, d, bn), lambda b, i, j: (layer, 0, jnp.minimum(j, n_in - 1))),
                  pl.BlockSpec((None, d, bn), lambda b, i, j: (layer, 0, jnp.maximum(j - n_in, 0)))],
        out_specs=pl.BlockSpec((1, bm, bn), lambda b, i, j: (b, i, j)),
        scratch_shapes=[pltpu.VMEM((bm, d), BF16)],
        compiler_params=pltpu.CompilerParams(
            dimension_semantics=("parallel", "parallel", "arbitrary"),
            vmem_limit_bytes=VMEM_LIMIT),
        name="mixer_in_proj",
    )(x, mod, w_in, w_gate)


def _mixer_kernel(proj_ref, halo_ref, x_ref, mod_ref, ca_ref, dww_ref, dwb_ref,
                  lnbg_ref, lnbb_ref, lncg_ref, lncb_ref, sw_ref, sb_ref,
                  woa_ref, wob_ref, woc_ref, bg_ref, wo_ref, ln1g_ref, ln1b_ref,
                  wr_ref, br_ref,
                  xo_ref, h2_ref, lg_ref,
                  abuf, bbuf, cbuf, sbuf, *, alpha):
    t = x_ref.shape[1]
    d = x_ref.shape[2]
    w = ca_ref.shape[1]
    n_heads = sw_ref.shape[0]
    hd = w // n_heads
    ka = ca_ref.shape[0]
    kb = dww_ref.shape[0]
    hal = CONV_HALO

    def pf(k):
        return proj_ref[0, :, k * w:(k + 1) * w].astype(F32)

    def hf(k):
        return halo_ref[0, :, k * w:(k + 1) * w].astype(F32)

    keep = jnp.where(pl.program_id(1) > 0, 1.0, 0.0).astype(F32)

    abuf[0:hal, :] = hf(0) * hf(2) * keep
    abuf[hal:hal + t, :] = pf(0) * pf(2)
    conv_a = ca_ref[0:1, :] * abuf[hal - ka + 1:hal - ka + 1 + t, :]
    for k in range(1, ka):
        conv_a = conv_a + ca_ref[k:k + 1, :] * abuf[hal - ka + 1 + k:hal - ka + 1 + k + t, :]
    y_a = (pf(1) * conv_a).astype(BF16)
    o_a = jnp.dot(y_a, woa_ref[...], preferred_element_type=F32)

    bbuf[0:hal, :] = hf(3) * _sigmoid(hf(4)) * keep
    bbuf[hal:hal + t, :] = pf(3) * _sigmoid(pf(4))
    rows = CONV_ROWS
    cols = sbuf.shape[2]
    span = sbuf.shape[1]
    for c0 in range(0, w, cols):
        for r in range(1, SUBLANES):
            sbuf[r - 1] = bbuf[r:r + span, c0:c0 + cols]
        for r0 in range(0, t, rows):
            acc = None
            for k in range(kb):
                q, r = divmod(hal - kb + 1 + k, SUBLANES)
                lo = r0 + SUBLANES * q
                if r == 0:
                    src = bbuf[lo:lo + rows, c0:c0 + cols]
                else:
                    src = sbuf[r - 1, lo:lo + rows, :]
                term = dww_ref[k:k + 1, c0:c0 + cols] * src
                acc = term if acc is None else acc + term
            cbuf[r0:r0 + rows, c0:c0 + cols] = acc
    y_b = _layer_norm(cbuf[...] + dwb_ref[...], lnbg_ref[...], lnbb_ref[...])
    y_b = (y_b * _sigmoid(y_b)).astype(BF16)
    o_b = jnp.dot(y_b, wob_ref[...], preferred_element_type=F32)

    v = _layer_norm(_gelu(pf(6)), lncg_ref[...], lncb_ref[...]).astype(BF16)
    ri = lax.broadcasted_iota(jnp.int32, (SGU_BLOCK, SGU_BLOCK), 0) // CHUNK
    ci = lax.broadcasted_iota(jnp.int32, (SGU_BLOCK, SGU_BLOCK), 1) // CHUNK
    causal = ri >= ci
    for g in range(n_heads):
        wm = jnp.where(causal, sw_ref[g], 0.0).astype(BF16)
        for blk in range(t // SGU_BLOCK):
            r0 = blk * SGU_BLOCK
            cbuf[r0:r0 + SGU_BLOCK, g * hd:(g + 1) * hd] = (
                jnp.dot(wm, v[r0:r0 + SGU_BLOCK, g * hd:(g + 1) * hd], preferred_element_type=F32)
                + sb_ref[:, g * hd:(g + 1) * hd])
    y_c = (_gelu(pf(5)) * cbuf[...]).astype(BF16)
    o_c = jnp.dot(y_c, woc_ref[...], preferred_element_type=F32)

    g0 = 7 * w

    def gate(k):
        z = proj_ref[0, :, g0 + k * d:g0 + (k + 1) * d].astype(F32) + bg_ref[:, k * d:(k + 1) * d]
        return _sigmoid(z)

    merged = (gate(0) * o_a + gate(1) * o_b + gate(2) * o_c).astype(BF16)
    mix = jnp.dot(merged, wo_ref[...], preferred_element_type=F32)
    xn = _layer_norm(alpha * x_ref[0] + mod_ref[0, 2:3, :] * mix, ln1g_ref[...], ln1b_ref[...])
    xo_ref[0] = xn

    h2 = xn * (1.0 + mod_ref[0, 4:5, :]) + mod_ref[0, 3:4, :]
    h2_hi = h2.astype(BF16)
    h2_ref[0] = _pack_halves(h2)
    h2_lo = (h2 - h2_hi.astype(F32)).astype(BF16)
    p_hi = jnp.dot(h2_hi, wr_ref[...], preferred_element_type=F32)
    p_lo = jnp.dot(h2_lo, wr_ref[:, 0:ROUTER_LANES], preferred_element_type=F32)
    lg_ref[0] = p_hi[:, 0:ROUTER_LANES] + p_hi[:, ROUTER_LANES:] + p_lo + br_ref[...]


def _const_spec(shape, index):
    return pl.BlockSpec(shape, index, pipeline_mode=pl.Buffered(1))


def _mixer_call(proj, x, mod, p, layer, alpha):
    bsz, seq, d = x.shape
    w = p["conv_a"].shape[-1]
    nc = proj.shape[-1]
    t = min(seq, SEQ_TILE)
    hal = CONV_HALO
    hb = t // hal
    n_heads = p["sgu_w"].shape[1]
    row = lambda a: a[layer].reshape(1, -1)
    grid = (bsz, seq // t)
    cst = lambda *shape: _const_spec(shape, lambda b, s: (0,) * len(shape))
    lyr = lambda *shape: _const_spec((None,) + shape, lambda b, s: (layer,) + (0,) * len(shape))
    in_specs = [
        pl.BlockSpec((1, t, nc), lambda b, s: (b, s, 0)),
        pl.BlockSpec((1, hal, 7 * w), lambda b, s: (b, jnp.maximum(s * hb - 1, 0), 0)),
        pl.BlockSpec((1, t, d), lambda b, s: (b, s, 0)),
        pl.BlockSpec((1, N_MOD, d), lambda b, s: (b, 0, 0)),
        lyr(p["conv_a"].shape[1], w),
        lyr(p["dw_b_w"].shape[1], w),
        cst(1, w), cst(1, w), cst(1, w), cst(1, w), cst(1, w),
        lyr(n_heads, SGU_BLOCK, SGU_BLOCK),
        cst(SGU_BLOCK, w),
        lyr(w, d), lyr(w, d), lyr(w, d),
        cst(1, 3 * d),
        lyr(d, d),
        cst(1, d), cst(1, d),
        cst(d, 2 * ROUTER_LANES), cst(1, ROUTER_LANES),
    ]
    out_specs = [
        pl.BlockSpec((1, t, d), lambda b, s: (b, s, 0)),
        pl.BlockSpec((1, t, d // 2), lambda b, s: (b, s, 0)),
        pl.BlockSpec((1, t, ROUTER_LANES), lambda b, s: (b, s, 0)),
    ]
    out_shape = [
        jax.ShapeDtypeStruct((bsz, seq, d), F32),
        jax.ShapeDtypeStruct((bsz, seq, d // 2), U32),
        jax.ShapeDtypeStruct((bsz, seq, ROUTER_LANES), F32),
    ]
    n_exp = p["w_router"].shape[-1]
    wr = jnp.pad(p["w_router"][layer], ((0, 0), (0, ROUTER_LANES - n_exp)))
    wr_hi = wr.astype(BF16)
    wr = jnp.concatenate([wr_hi, (wr - wr_hi.astype(F32)).astype(BF16)], axis=1)
    br = jnp.pad(p["b_router"][layer], (0, ROUTER_LANES - n_exp)).reshape(1, -1)
    sb = jnp.repeat(p["sgu_b"][layer].T, w // n_heads, axis=1)
    return pl.pallas_call(
        functools.partial(_mixer_kernel, alpha=alpha),
        out_shape=out_shape,
        grid=grid,
        in_specs=in_specs,
        out_specs=out_specs,
        scratch_shapes=[pltpu.VMEM((t + hal, w), F32), pltpu.VMEM((t + hal, w), F32),
                        pltpu.VMEM((t, w), F32),
                        pltpu.VMEM((SUBLANES - 1, t + hal - SUBLANES, min(w, CONV_COLS)), F32)],
        compiler_params=pltpu.CompilerParams(dimension_semantics=("parallel", "parallel"),
                                             vmem_limit_bytes=VMEM_LIMIT),
        name="mixer",
    )(proj, proj, x, mod, p["conv_a"], p["dw_b_w"], row(p["dw_b_b"]),
      row(p["ln_b_g"]), row(p["ln_b_b"]), row(p["ln_c_g"]), row(p["ln_c_b"]),
      p["sgu_w"], sb, p["w_out_a"], p["w_out_b"], p["w_out_c"], row(p["b_gate"]),
      p["w_o"], row(p["ln1_g"]), row(p["ln1_b"]), wr, br)


def _moe_kernel(be_ref, first_ref, nxt_ref, nb_ref, rcur_ref, rnxt_ref, h_hbm, wgu_hbm, bgu_ref,
                wd_hbm, bd_ref, o_ref,
                gu_stage, d_stage, wgu_bf, wd_bf, x_even, x_odd, gu_buf, wsem, xsem, *, layer):
    f = wd_bf.shape[0]
    tm, half = x_even.shape
    i = pl.program_id(0)

    def weight_copies(expert):
        return (pltpu.make_async_copy(wgu_hbm.at[layer, expert], gu_stage, wsem.at[0]),
                pltpu.make_async_copy(wd_hbm.at[layer, expert], d_stage, wsem.at[1]))

    def row_copy(r_ref, r, dst, s):
        return pltpu.make_async_copy(h_hbm.at[pl.ds(r_ref[0, 0, r], 1)], dst.at[pl.ds(r, 1)],
                                     xsem.at[s])

    def start_rows_inline(r_ref, dst, s):
        for r in range(tm):
            row_copy(r_ref, r, dst, s).start()

    def start_rows_loop(r_ref, dst, s):
        def body(r, carry):
            row_copy(r_ref, r, dst, s).start()
            return carry
        lax.fori_loop(0, tm, body, 0)

    def wait_rows(dst, s):
        pltpu.make_async_copy(h_hbm.at[pl.ds(0, tm)], dst, xsem.at[s]).wait()

    @pl.when(i == 0)
    def _():
        for cp in weight_copies(be_ref[0]):
            cp.start(priority=1)
        start_rows_loop(rcur_ref, x_even, 0)

    @pl.when(first_ref[i] == 1)
    def _():
        for cp in weight_copies(be_ref[i]):
            cp.wait()
        wgu_bf[...] = gu_stage[...].astype(BF16)
        wd_bf[...] = d_stage[...].astype(BF16)

        @pl.when(nxt_ref[i] >= 0)
        def _():
            for cp in weight_copies(nxt_ref[i]):
                cp.start(priority=1)

    def step(cur, cur_sem, nxt, nxt_sem):
        wait_rows(cur, cur_sem)

        @pl.when(i < nb_ref[0])
        def _():
            start_rows_inline(rnxt_ref, nxt, nxt_sem)
            hi, _ = _unpack_halves(cur[...])
            gu_buf[...] = jnp.dot(hi.astype(BF16), wgu_bf[0:half, :],
                                  preferred_element_type=F32) + bgu_ref[...]

        @pl.when(i < nb_ref[0])
        def _():
            _, lo = _unpack_halves(cur[...])
            gu_buf[...] += jnp.dot(lo.astype(BF16), wgu_bf[half:, :], preferred_element_type=F32)

        @pl.when(i < nb_ref[0])
        def _():
            gu = gu_buf[...]
            gate = jnp.minimum(gu[:, :f], SWIGLU_LIMIT)
            up = jnp.clip(gu[:, f:], -SWIGLU_LIMIT, SWIGLU_LIMIT)
            act = (up + 1.0) * (gate * _sigmoid(SWIGLU_ALPHA * gate))
            y = jnp.dot(act.astype(BF16), wd_bf[...], preferred_element_type=F32) + bd_ref[...]
            o_ref[...] = _pack_halves(y)

        @pl.when(i >= nb_ref[0])
        def _():
            start_rows_loop(rnxt_ref, nxt, nxt_sem)
            o_ref[...] = jnp.zeros(o_ref.shape, o_ref.dtype)

        @pl.when(i == pl.num_programs(0) - 1)
        def _():
            wait_rows(nxt, nxt_sem)

    @pl.when(lax.rem(i, 2) == 0)
    def _():
        step(x_even, 0, x_odd, 1)

    @pl.when(lax.rem(i, 2) == 1)
    def _():
        step(x_odd, 1, x_even, 0)


def _moe_call(plan, row_token, h2p, w_gu, b_gu, w_down, b_down, layer, tm):
    half = h2p.shape[1]
    d = 2 * half
    n_rows = row_token.shape[0]
    nb = n_rows // tm
    n_exp, _, f2 = w_gu.shape[1:]
    f = f2 // 2
    rsteps = row_token.reshape(nb, 1, tm)
    smem = lambda im: pl.BlockSpec((1, 1, tm), im, memory_space=pltpu.SMEM)
    grid_spec = pltpu.PrefetchScalarGridSpec(
        num_scalar_prefetch=4,
        grid=(nb,),
        in_specs=[
            smem(lambda i, *_: (i, 0, 0)),
            smem(lambda i, *_: (jnp.minimum(i + 1, nb - 1), 0, 0)),
            pl.BlockSpec(memory_space=pl.ANY),
            pl.BlockSpec(memory_space=pl.ANY),
            pl.BlockSpec((None, None, 1, f2), lambda i, be, *_: (layer, be[i], 0, 0)),
            pl.BlockSpec(memory_space=pl.ANY),
            pl.BlockSpec((None, None, 1, d), lambda i, be, *_: (layer, be[i], 0, 0)),
        ],
        out_specs=pl.BlockSpec((tm, half), lambda i, *_: (i, 0)),
        scratch_shapes=[pltpu.VMEM((d, f2), F32), pltpu.VMEM((f, d), F32),
                        pltpu.VMEM((d, f2), BF16), pltpu.VMEM((f, d), BF16),
                        pltpu.VMEM((tm, half), U32), pltpu.VMEM((tm, half), U32),
                        pltpu.VMEM((tm, f2), F32),
                        pltpu.SemaphoreType.DMA((2,)), pltpu.SemaphoreType.DMA((2,))],
    )
    return pl.pallas_call(
        functools.partial(_moe_kernel, layer=layer),
        out_shape=jax.ShapeDtypeStruct((n_rows, half), U32),
        grid_spec=grid_spec,
        compiler_params=pltpu.CompilerParams(dimension_semantics=("arbitrary",),
                                             vmem_limit_bytes=VMEM_LIMIT),
        name="moe_experts",
    )(*plan, rsteps, rsteps, h2p, w_gu, b_gu.reshape(b_gu.shape[0], n_exp, 1, f2),
      w_down, b_down.reshape(b_down.shape[0], n_exp, 1, d))


def _ln2_kernel(dcur_ref, dnxt_ref, x_ref, yb_ref, tw_ref, mod_ref, g_ref, b_ref, o_ref,
                buf_even, buf_odd, sem, *, alpha):
    t = x_ref.shape[1]
    n_rows_step = TOP_K * t
    i = pl.program_id(0)

    def start_gather(d_ref, dst, s):
        for r in range(n_rows_step):
            pltpu.make_async_copy(yb_ref.at[pl.ds(d_ref[0, 0, r], 1)], dst.at[pl.ds(r, 1)],
                                  sem.at[s]).start(priority=r % N_DMA_PRIORITIES)

    def wait_gather(dst, s):
        pltpu.make_async_copy(yb_ref.at[pl.ds(0, n_rows_step)], dst, sem.at[s]).wait()

    def combine(src):
        tw = tw_ref[0]
        y_hi = y_lo = None
        for k in range(TOP_K):
            hi, lo = _unpack_halves(src[k * t:(k + 1) * t, :])
            wk = tw[:, k:k + 1]
            y_hi = wk * hi if y_hi is None else y_hi + wk * hi
            y_lo = wk * lo if y_lo is None else y_lo + wk * lo
        y = jnp.concatenate([y_hi, y_lo], axis=1)
        o_ref[0] = _layer_norm(alpha * x_ref[0] + mod_ref[0, 5:6, :] * y, g_ref[...], b_ref[...])

    @pl.when(i == 0)
    def _():
        start_gather(dcur_ref, buf_even, 0)

    def step(cur, cur_sem, nxt, nxt_sem):
        wait_gather(cur, cur_sem)
        start_gather(dnxt_ref, nxt, nxt_sem)
        combine(cur)

        @pl.when(i == pl.num_programs(0) - 1)
        def _():
            wait_gather(nxt, nxt_sem)

    @pl.when(lax.rem(i, 2) == 0)
    def _():
        step(buf_even, 0, buf_odd, 1)

    @pl.when(lax.rem(i, 2) == 1)
    def _():
        step(buf_odd, 1, buf_even, 0)


def _ln2_call(x, yb, dest, top_w, mod, g, b, alpha):
    bsz, seq, d = x.shape
    t = min(seq, SEQ_TILE)
    spb = seq // t
    nst = bsz * spb
    dsteps = dest.reshape(nst, t, TOP_K).transpose(0, 2, 1).reshape(nst, 1, TOP_K * t)
    smem = lambda im: pl.BlockSpec((1, 1, TOP_K * t), im, memory_space=pltpu.SMEM)
    tile = lambda last: pl.BlockSpec((1, t, last), lambda i: (i // spb, i % spb, 0))
    return pl.pallas_call(
        functools.partial(_ln2_kernel, alpha=alpha),
        out_shape=jax.ShapeDtypeStruct((bsz, seq, d), F32),
        grid=(nst,),
        in_specs=[smem(lambda i: (i, 0, 0)),
                  smem(lambda i: (jnp.minimum(i + 1, nst - 1), 0, 0)),
                  tile(d),
                  pl.BlockSpec(memory_space=pl.ANY),
                  tile(TOP_K),
                  pl.BlockSpec((1, N_MOD, d), lambda i: (i // spb, 0, 0)),
                  pl.BlockSpec((1, d), lambda i: (0, 0)),
                  pl.BlockSpec((1, d), lambda i: (0, 0))],
        out_specs=tile(d),
        scratch_shapes=[pltpu.VMEM((TOP_K * t, d // 2), U32), pltpu.VMEM((TOP_K * t, d // 2), U32),
                        pltpu.SemaphoreType.DMA((2,))],
        compiler_params=pltpu.CompilerParams(dimension_semantics=("arbitrary",),
                                             vmem_limit_bytes=VMEM_LIMIT),
        name="moe_combine_ln",
    )(dsteps, dsteps, x, yb, top_w, mod, g.reshape(1, d), b.reshape(1, d))


def _router_kernel(lg_ref, o_ref, cnt_ref, carry, *, n_exp):
    tr = lg_ref.shape[0]

    @pl.when(pl.program_id(0) == 0)
    def _():
        carry[...] = jnp.zeros(carry.shape, carry.dtype)

    lane = lax.broadcasted_iota(jnp.int32, (tr, LANES), 1)
    lane_f = lane.astype(F32)
    neg = jnp.float32(-jnp.inf)
    lg = jnp.where(lane < n_exp, lg_ref[...], neg)
    sels, tops, ids = [], [], []
    for _ in range(TOP_K):
        m = jnp.max(lg, axis=-1, keepdims=True)
        idx = jnp.min(jnp.where(lg == m, lane_f, float(LANES)), axis=-1, keepdims=True)
        sel = lane_f == idx
        lg = jnp.where(sel, neg, lg)
        sels.append(sel)
        tops.append(m)
        ids.append(idx)
    exps = [jnp.exp(m - tops[0]) for m in tops]
    denom = exps[0]
    for e in exps[1:]:
        denom = denom + e
    onehot = jnp.where(sels[0], 1.0, 0.0)
    for sel in sels[1:]:
        onehot = onehot + jnp.where(sel, 1.0, 0.0)
    rr = lax.broadcasted_iota(jnp.int32, (tr, tr), 0)
    cc = lax.broadcasted_iota(jnp.int32, (tr, tr), 1)
    earlier = jnp.where(cc < rr, 1.0, 0.0).astype(BF16)
    prefix = jnp.dot(earlier, onehot.astype(BF16), preferred_element_type=F32) + carry[0:1, :]
    out = jnp.zeros((tr, LANES), F32)
    for k in range(TOP_K):
        rank = jnp.sum(jnp.where(sels[k], prefix, 0.0), axis=-1, keepdims=True)
        out = jnp.where(lane == k, exps[k] / denom, out)
        out = jnp.where(lane == TOP_K + k, ids[k], out)
        out = jnp.where(lane == 2 * TOP_K + k, rank, out)
    o_ref[...] = out
    carry[0:1, :] = carry[0:1, :] + jnp.sum(onehot, axis=0, keepdims=True)
    cnt_ref[...] = carry[...]


def _router_call(logits, n_exp):
    n = logits.shape[0]
    tr = min(n, ROUTER_ROWS)
    return pl.pallas_call(
        functools.partial(_router_kernel, n_exp=n_exp),
        out_shape=[jax.ShapeDtypeStruct((n, LANES), F32),
                   jax.ShapeDtypeStruct((SUBLANES, LANES), F32)],
        grid=(n // tr,),
        in_specs=[pl.BlockSpec((tr, LANES), lambda i: (i, 0))],
        out_specs=[pl.BlockSpec((tr, LANES), lambda i: (i, 0)),
                   pl.BlockSpec((SUBLANES, LANES), lambda i: (0, 0))],
        scratch_shapes=[pltpu.VMEM((SUBLANES, LANES), F32)],
        compiler_params=pltpu.CompilerParams(dimension_semantics=("arbitrary",),
                                             vmem_limit_bytes=VMEM_LIMIT),
        name="router_topk",
    )(logits)


def _route(logits, n_exp, tm):
    n = logits.shape[0]
    nk = n * TOP_K
    routed, cnt = _router_call(logits, n_exp)
    top_w = routed[:, 0:TOP_K]
    top_idx = routed[:, TOP_K:2 * TOP_K].astype(jnp.int32)
    rank = routed[:, 2 * TOP_K:3 * TOP_K].astype(jnp.int32)
    counts = cnt[0, :n_exp].astype(jnp.int32)
    padded = (counts + tm - 1) // tm * tm
    pend = jnp.cumsum(padded)
    pstart = pend - padded
    dest = pstart[top_idx] + rank
    n_rows = (nk + tm - 1) // tm * tm + n_exp * tm
    n_blocks = n_rows // tm
    n_used = (pend[-1] // tm).astype(jnp.int32)
    blk = jnp.arange(n_blocks, dtype=jnp.int32)
    blk_start = jnp.minimum(blk, n_used - 1) * tm
    block_e = jnp.minimum(jnp.sum(pend[None, :] <= blk_start[:, None], axis=1), n_exp - 1).astype(jnp.int32)
    tok = jnp.broadcast_to(jnp.arange(n, dtype=jnp.int32)[:, None], (n, TOP_K))
    _, tok_by_row = lax.sort((dest.reshape(-1), tok.reshape(-1)), num_keys=1)
    start = jnp.cumsum(counts) - counts
    rows = jnp.arange(n_rows, dtype=jnp.int32).reshape(n_blocks, tm)
    in_expert = rows - pstart[block_e][:, None]
    occupied = (in_expert < counts[block_e][:, None]) & (blk < n_used)[:, None]
    src = jnp.clip(start[block_e][:, None] + in_expert, 0, nk - 1)
    row_token = jnp.where(occupied, tok_by_row[src], rows % n).reshape(n_rows)
    first = ((blk < n_used) & ((blk == 0) | (block_e != jnp.roll(block_e, 1)))).astype(jnp.int32)
    ids = jnp.where(counts > 0, jnp.arange(n_exp, dtype=jnp.int32), n_exp)
    after = jnp.concatenate([lax.cummin(ids, axis=0, reverse=True)[1:], jnp.full((1,), n_exp, jnp.int32)])
    nxt = jnp.where(after < n_exp, after, -1)[block_e].astype(jnp.int32)
    plan = (block_e, first, nxt, n_used.reshape(1))
    return top_w, dest.astype(jnp.int32), row_token, plan


def kernel(x, c, w_ada, b_ada, ada_table, w_in, conv_a, dw_b_w, dw_b_b, ln_b_g, ln_b_b, ln_c_g, ln_c_b, sgu_w, sgu_b, w_out_a, w_out_b, w_out_c, w_gate, b_gate, w_o, ln1_g, ln1_b, w_router, b_router, w_gu, b_gu, w_down, b_down, ln2_g, ln2_b):
    bsz, seq, d = x.shape
    depth = w_in.shape[0]
    n_exp = w_router.shape[-1]
    w = conv_a.shape[-1]
    alpha = (2.0 * depth) ** 0.25
    n = bsz * seq

    mod_shared = _ada_call(c, w_ada, b_ada).reshape(bsz, N_MOD, d)
    w_in_b = w_in.astype(BF16)
    w_gate_b = w_gate.astype(BF16)
    p = dict(conv_a=conv_a, dw_b_w=dw_b_w, dw_b_b=dw_b_b, ln_b_g=ln_b_g, ln_b_b=ln_b_b,
             ln_c_g=ln_c_g, ln_c_b=ln_c_b, sgu_w=sgu_w, sgu_b=sgu_b,
             w_out_a=w_out_a.astype(BF16), w_out_b=w_out_b.astype(BF16),
             w_out_c=w_out_c.astype(BF16), b_gate=b_gate, w_o=w_o.astype(BF16),
             ln1_g=ln1_g, ln1_b=ln1_b, w_router=w_router, b_router=b_router)
    tm = MOE_ROWS

    for l in range(depth):
        mod = mod_shared + ada_table[l][None]
        proj = _proj_call(x, mod, w_in_b, w_gate_b, l, w)
        x, h2, logits = _mixer_call(proj, x, mod, p, l, alpha)

        top_w, dest, row_token, plan = _route(logits.reshape(n, -1), n_exp, tm)
        yb = _moe_call(plan, row_token, h2.reshape(n, d // 2), w_gu, b_gu, w_down, b_down, l, tm)
        x = _ln2_call(x, yb, dest, top_w.reshape(bsz, seq, TOP_K), mod, ln2_g[l], ln2_b[l], alpha)
    return x
```
